```python
import jax, jax.numpy as jnp
from jax import lax
import numpy as np

D_MODEL = 1024
BATCH = 8
SEQ = 4096
DEPTH = 1

N_META = 16
EPS = 1e-6
NEG = -1e30
M_HEADS = 8
M_DQK = 64
M_DV = 128
M_CHUNK = 64
A_HEADS = 8
A_NOPE = 128
A_ROPE = 64
A_DV = 128
Q_LORA = 256
KV_LORA = 128
ROPE_THETA = 10000.0
Q_BLOCK = 128
N_EXPERTS = 32
TOP_K = 4
D_FF = 1024
SWIGLU_ALPHA = 1.702
SWIGLU_LIMIT = 7.0
MOE_BLOCK = 128
IN_SIZES = (M_HEADS * M_DQK, M_HEADS * M_DQK, M_HEADS * M_DV, M_HEADS * M_DV,
            2 * M_HEADS, 2 * M_HEADS, Q_LORA, KV_LORA, A_ROPE, 2 * D_MODEL)
D_IN = sum(IN_SIZES)
IN_OFFSETS = tuple(int(s) for s in np.cumsum(IN_SIZES)[:-1])

kernel_name = 'hybrid_mlstm_mla_moe_encoder'


def rms_norm(x, g):
    xf = x.astype(jnp.float32)
    y = xf * lax.rsqrt(jnp.mean(xf * xf, axis=-1, keepdims=True) + EPS)
    return (y * g.astype(jnp.float32)).astype(x.dtype)


def rope(x, pos):
    half = x.shape[-1] // 2
    freqs = ROPE_THETA ** (-jnp.arange(half, dtype=jnp.float32) / half)
    ang = pos.astype(jnp.float32)[:, None] * freqs[None, :]
    cos, sin = jnp.cos(ang).astype(x.dtype), jnp.sin(ang).astype(x.dtype)
    x1, x2 = x[..., :half], x[..., half:]
    return jnp.concatenate([x1 * cos - x2 * sin, x1 * sin + x2 * cos], axis=-1)


def mlstm_scan(q, k, v, li, lf):
    B, H, Lp, _ = q.shape
    nc = Lp // M_CHUNK

    def chunks(a):
        a = a.reshape(B, H, nc, M_CHUNK, *a.shape[3:])
        return jnp.moveaxis(a, 2, 0)

    tril = jnp.tril(jnp.ones((M_CHUNK, M_CHUNK), dtype=bool))

    def step(carry, inp):
        C, n, m = carry
        qc, kc, vc, lic, lfc = inp
        b = jnp.cumsum(lfc, axis=-1)
        dmat = jnp.where(tril, b[..., :, None] - b[..., None, :] + lic[..., None, :], NEG)
        m_inter = b + m[..., None]
        m_t = jnp.maximum(m_inter, dmat.max(axis=-1))
        w = jnp.exp(dmat - m_t[..., None]) * jnp.einsum('bhtd,bhsd->bhts', qc, kc)
        s_inter = jnp.exp(m_inter - m_t)
        num = s_inter[..., None] * jnp.einsum('bhtd,bhdv->bhtv', qc, C) + jnp.einsum('bhts,bhsv->bhtv', w, vc)
        den = s_inter * jnp.einsum('bhtd,bhd->bht', qc, n) + w.sum(axis=-1)
        h = num / jnp.maximum(jnp.abs(den), jnp.exp(-m_t))[..., None]
        b_last = b[..., -1]
        g = b_last[..., None] - b + lic
        m_new = jnp.maximum(b_last + m, g.max(axis=-1))
        decay = jnp.exp(b_last + m - m_new)
        wk = jnp.exp(g - m_new[..., None])
        C_new = decay[..., None, None] * C + jnp.einsum('bhs,bhsd,bhsv->bhdv', wk, kc, vc)
        n_new = decay[..., None] * n + jnp.einsum('bhs,bhsd->bhd', wk, kc)
        return (C_new, n_new, m_new), h

    init = (jnp.zeros((B, H, q.shape[-1], v.shape[-1]), jnp.float32),
            jnp.zeros((B, H, q.shape[-1]), jnp.float32),
            jnp.zeros((B, H), jnp.float32))
    _, hs = lax.scan(step, init, (chunks(q), chunks(k), chunks(v), chunks(li), chunks(lf)))
    return jnp.moveaxis(hs, 0, 2).reshape(B, H, Lp, v.shape[-1])


def mlstm_branch(q, k, v, o_pre, i_pre, f_pre, b_i, b_f, g_norm):
    B, L, _ = q.shape
    pad = M_CHUNK - N_META
    Lp = L + pad

    def heads(a, d):
        a = a.astype(jnp.float32).reshape(B, L, M_HEADS, d)
        a = jnp.pad(a, ((0, 0), (pad, 0), (0, 0), (0, 0)))
        return a.transpose(0, 2, 1, 3)

    qh = heads(q, M_DQK) * (M_DQK ** -0.5)
    kh = heads(k, M_DQK)
    vh = heads(v, M_DV)
    valid = (jnp.arange(Lp) >= pad)[None, :, None, None]
    i_all = jnp.pad(i_pre.astype(jnp.float32).reshape(B, L, 2, M_HEADS) + b_i.astype(jnp.float32),
                    ((0, 0), (pad, 0), (0, 0), (0, 0)))
    f_all = jnp.pad(f_pre.astype(jnp.float32).reshape(B, L, 2, M_HEADS) + b_f.astype(jnp.float32),
                    ((0, 0), (pad, 0), (0, 0), (0, 0)))
    li = jnp.where(valid, i_all, NEG).transpose(0, 2, 3, 1)
    lf = jnp.where(valid, jax.nn.log_sigmoid(f_all), 0.0).transpose(0, 2, 3, 1)
    flip = lambda a: jnp.flip(a, axis=2)
    h_fwd = mlstm_scan(qh, kh, vh, li[:, 0], lf[:, 0])
    h_bwd = flip(mlstm_scan(flip(qh), flip(kh), flip(vh), flip(li[:, 1]), flip(lf[:, 1])))
    h = (h_fwd + h_bwd)[:, :, pad:]
    h = rms_norm(h, g_norm[:, None, :])
    h = h.transpose(0, 2, 1, 3).reshape(B, L, M_HEADS * M_DV).astype(q.dtype)
    return h * jax.nn.sigmoid(o_pre.astype(jnp.float32)).astype(q.dtype)


def mla_branch(cq, ckv, k_rope, w_uq, w_ukv, g_q, g_kv, pos):
    B, L, _ = cq.shape
    q = jnp.einsum('blr,rhd->bhld', rms_norm(cq, g_q), w_uq)
    q_nope, q_rope = q[..., :A_NOPE], rope(q[..., A_NOPE:], pos)
    kv = jnp.einsum('blr,rhd->bhld', rms_norm(ckv, g_kv), w_ukv)
    k_nope, v = kv[..., :A_NOPE], kv[..., A_NOPE:]
    k_r = rope(k_rope, pos)
    scale = (A_NOPE + A_ROPE) ** -0.5

    def attend(qn, qr):
        s = jnp.einsum('bhqd,bhkd->bhqk', qn, k_nope) + jnp.einsum('bhqd,bkd->bhqk', qr, k_r)
        p = jax.nn.softmax(s.astype(jnp.float32) * scale, axis=-1).astype(v.dtype)
        return jnp.einsum('bhqk,bhkd->bhqd', p, v)

    out_meta = attend(q_nope[:, :, :N_META], q_rope[:, :, :N_META])
    nb = (L - N_META) // Q_BLOCK

    def blocks(a):
        a = a[:, :, N_META:].reshape(B, A_HEADS, nb, Q_BLOCK, a.shape[-1])
        return jnp.moveaxis(a, 2, 0)

    out_real = lax.map(lambda qs: attend(qs[0], qs[1]), (blocks(q_nope), blocks(q_rope)))
    out_real = jnp.moveaxis(out_real, 0, 2).reshape(B, A_HEADS, L - N_META, A_DV)
    out = jnp.concatenate([out_meta, out_real], axis=2)
    return out.transpose(0, 2, 1, 3).reshape(B, L, A_HEADS * A_DV)


def moe(x, w_router, b_router, w1, b1, w2, b2):
    B, L, D = x.shape
    xt = x.reshape(-1, D)
    T = xt.shape[0]
    logits = (xt @ w_router).astype(jnp.float32) + b_router.astype(jnp.float32)
    top_val, top_idx = lax.top_k(logits, TOP_K)
    gates = jax.nn.softmax(top_val, axis=-1)
    n = T * TOP_K
    flat_e = top_idx.reshape(-1)
    flat_tok = jnp.repeat(jnp.arange(T, dtype=jnp.int32), TOP_K)
    flat_w = gates.reshape(-1)
    order = jnp.argsort(flat_e)
    sorted_e = flat_e[order]
    counts = jnp.bincount(flat_e, length=N_EXPERTS)
    start = jnp.cumsum(counts) - counts
    padded = (counts + MOE_BLOCK - 1) // MOE_BLOCK * MOE_BLOCK
    padded_end = jnp.cumsum(padded)
    padded_start = padded_end - padded
    dest = padded_start[sorted_e] + jnp.arange(n, dtype=jnp.int32) - start[sorted_e]
    nb = -(-n // MOE_BLOCK) + N_EXPERTS
    buf_tok = jnp.full((nb * MOE_BLOCK,), T, jnp.int32).at[dest].set(flat_tok[order])
    buf_w = jnp.zeros((nb * MOE_BLOCK,), jnp.float32).at[dest].set(flat_w[order])
    block_e = jnp.minimum(jnp.searchsorted(padded_end, jnp.arange(nb, dtype=jnp.int32) * MOE_BLOCK, side='right'),
                          N_EXPERTS - 1)
    x_pad = jnp.concatenate([xt, jnp.zeros((1, D), xt.dtype)], axis=0)

    def expert_block(args):
        toks, wts, e = args
        hid = x_pad[toks] @ w1[e] + b1[e]
        glu = jnp.minimum(hid[..., :D_FF], SWIGLU_LIMIT)
        lin = jnp.clip(hid[..., D_FF:], -SWIGLU_LIMIT, SWIGLU_LIMIT)
        act = glu * jax.nn.sigmoid(SWIGLU_ALPHA * glu) * (lin + 1)
        return (act @ w2[e] + b2[e]).astype(jnp.float32) * wts[:, None]

    yb = lax.map(expert_block, (buf_tok.reshape(nb, MOE_BLOCK), buf_w.reshape(nb, MOE_BLOCK), block_e))
    y = jnp.zeros((T + 1, D), jnp.float32).at[buf_tok].add(yb.reshape(-1, D))[:T]
    return y.astype(x.dtype).reshape(B, L, D)


def setup_inputs(seed: int = 0) -> dict:
    key = jax.random.key(seed)
    ks = jax.random.split(key, 22)
    f32 = jnp.float32

    def nrm(k, shape, scale):
        return jax.random.normal(k, shape, f32) * scale

    def gain(k, shape):
        return 1.0 + 0.02 * jax.random.normal(k, shape, f32)

    f_bias = jnp.linspace(3.0, 6.0, M_HEADS, dtype=f32)[None, None, :]
    return {
        'x': nrm(ks[0], (BATCH, SEQ, D_MODEL), 1.0),
        'meta_tokens': nrm(ks[1], (N_META, D_MODEL), 1.0),
        'norm_mix': gain(ks[2], (DEPTH, D_MODEL)),
        'w_in': nrm(ks[3], (DEPTH, D_MODEL, D_IN), D_MODEL ** -0.5),
        'm_b_i': nrm(ks[4], (DEPTH, 2, M_HEADS), 0.1),
        'm_b_f': f_bias + nrm(ks[5], (DEPTH, 2, M_HEADS), 0.1),
        'm_norm': gain(ks[6], (DEPTH, M_HEADS, M_DV)),
        'w_m_proj': nrm(ks[7], (DEPTH, M_HEADS * M_DV, D_MODEL), (M_HEADS * M_DV) ** -0.5),
        'a_w_uq': nrm(ks[8], (DEPTH, Q_LORA, A_HEADS, A_NOPE + A_ROPE), Q_LORA ** -0.5),
        'a_w_ukv': nrm(ks[9], (DEPTH, KV_LORA, A_HEADS, A_NOPE + A_DV), KV_LORA ** -0.5),
        'a_norm_q': gain(ks[10], (DEPTH, Q_LORA)),
        'a_norm_kv': gain(ks[11], (DEPTH, KV_LORA)),
        'w_a_proj': nrm(ks[12], (DEPTH, A_HEADS * A_DV, D_MODEL), (A_HEADS * A_DV) ** -0.5),
        'w_out': nrm(ks[13], (DEPTH, D_MODEL, D_MODEL), D_MODEL ** -0.5),
        'norm_ffn': gain(ks[14], (DEPTH, D_MODEL)),
        'w_router': nrm(ks[15], (DEPTH, D_MODEL, N_EXPERTS), D_MODEL ** -0.5),
        'b_router': nrm(ks[16], (DEPTH, N_EXPERTS), 0.01),
        'w1': nrm(ks[17], (DEPTH, N_EXPERTS, D_MODEL, 2 * D_FF), D_MODEL ** -0.5),
        'b1': nrm(ks[18], (DEPTH, N_EXPERTS, 2 * D_FF), 0.01),
        'w2': nrm(ks[19], (DEPTH, N_EXPERTS, D_FF, D_MODEL), D_FF ** -0.5),
        'b2': nrm(ks[20], (DEPTH, N_EXPERTS, D_MODEL), 0.01),
        'norm_final': gain(ks[21], (D_MODEL,)),
    }


def reference(x, meta_tokens, norm_mix, w_in, m_b_i, m_b_f, m_norm, w_m_proj,
              a_w_uq, a_w_ukv, a_norm_q, a_norm_kv, w_a_proj, w_out,
              norm_ffn, w_router, b_router, w1, b1, w2, b2, norm_final):
    B = x.shape[0]
    meta = jnp.broadcast_to(meta_tokens.astype(x.dtype)[None], (B, N_META, D_MODEL))
    h = jnp.concatenate([meta, x], axis=1)
    pos = jnp.arange(h.shape[1], dtype=jnp.int32)
    for l in range(DEPTH):
        u = rms_norm(h, norm_mix[l])
        z = u @ w_in[l]
        m_q, m_k, m_v, m_o, m_i, m_f, a_cq, a_ckv, a_kr, gate_pre = jnp.split(z, list(IN_OFFSETS), axis=-1)
        y_m = mlstm_branch(m_q, m_k, m_v, m_o, m_i, m_f, m_b_i[l], m_b_f[l], m_norm[l]) @ w_m_proj[l]
        y_a = mla_branch(a_cq, a_ckv, a_kr, a_w_uq[l], a_w_ukv[l], a_norm_q[l], a_norm_kv[l], pos) @ w_a_proj[l]
        gates = jax.nn.sigmoid(gate_pre.astype(jnp.float32)).astype(h.dtype)
        mixed = gates[..., :D_MODEL] * y_m + gates[..., D_MODEL:] * y_a
        h = h + mixed @ w_out[l]
        h = h + moe(rms_norm(h, norm_ffn[l]), w_router[l], b_router[l], w1[l], b1[l], w2[l], b2[l])
    return rms_norm(h, norm_final)[:, N_META:]
```

```python
import functools

import jax
import jax.numpy as jnp
import numpy as np
from jax import lax
from jax.experimental import pallas as pl
from jax.experimental.pallas import tpu as pltpu

N_META = 16
EPS = 1e-6
NEG = -1e30
M_HEADS = 8
M_DQK = 64
M_DV = 128
A_HEADS = 8
A_NOPE = 128
A_ROPE = 64
A_DV = 128
Q_LORA = 256
KV_LORA = 128
ROPE_THETA = 10000.0
N_EXPERTS = 32
TOP_K = 4
D_FF = 1024
SWIGLU_ALPHA = 1.702
SWIGLU_LIMIT = 7.0

LANES = 128
N_GATE_COLS = 4 * M_HEADS
ATT_SCALE = (A_NOPE + A_ROPE) ** -0.5
VMEM_LIMIT = 56 * 1024 * 1024

F32 = jnp.float32
BF16 = jnp.bfloat16


def _mm(a, b):
    return jnp.dot(a, b, preferred_element_type=F32)


def _mm_nt(a, b):
    return lax.dot_general(a, b, (((1,), (1,)), ((), ())), preferred_element_type=F32)


def _mm_tn(a, b):
    return lax.dot_general(a, b, (((0,), (0,)), ((), ())), preferred_element_type=F32)


def _mm_exact(a, b):
    return jnp.dot(a, b, preferred_element_type=F32, precision=lax.Precision.HIGHEST)


def _rms(x, g):
    return x * lax.rsqrt(jnp.mean(x * x, axis=-1, keepdims=True) + EPS) * g


def _log_sigmoid(x):
    return jnp.minimum(x, 0.0) - jnp.log1p(jnp.exp(-jnp.abs(x)))


def _sigmoid(x):
    return 1.0 / (1.0 + jnp.exp(-x))


def _resident(shape):
    nd = len(shape)
    return pl.BlockSpec(shape, lambda *_: (0,) * nd, pipeline_mode=pl.Buffered(1))


def _in_proj_kernel(x_ref, cos_ref, sin_ref, gmix_ref, wqk_ref, wvo_ref, wgate_ref, wlat_ref,
                    wif_ref, wift_ref, bif_ref, bift_ref, gq_ref, gkv_ref, wuq_ref, wukv_ref,
                    mq_ref, mk_ref, mv_ref, so_ref, sg_ref, gl_ref, glt_ref,
                    qn_ref, qr_ref, kn_ref, va_ref, kr_ref):
    d_qk = M_HEADS * M_DQK
    d_v = M_HEADS * M_DV
    ub = _rms(x_ref[...], gmix_ref[...]).astype(BF16)

    qk = _mm(ub, wqk_ref[...])
    mq_ref[...] = qk[:, :d_qk].astype(BF16)
    mk_ref[...] = qk[:, d_qk:].astype(BF16)
    vo = _mm(ub, wvo_ref[...])
    mv_ref[...] = vo[:, :d_v].astype(BF16)
    so_ref[...] = _sigmoid(vo[:, d_v:]).astype(BF16)
    sg_ref[...] = _sigmoid(_mm(ub, wgate_ref[...])).astype(BF16)

    gi = _mm(ub, wif_ref[...]) + bif_ref[...]
    col = lax.broadcasted_iota(jnp.int32, gi.shape, 1)
    gl_ref[...] = jnp.where(col < 2 * M_HEADS, gi, _log_sigmoid(gi))
    git = _mm_nt(wift_ref[...], ub) + bift_ref[...]
    row = lax.broadcasted_iota(jnp.int32, git.shape, 0)
    glt_ref[...] = jnp.where(row < 2 * M_HEADS, git, _log_sigmoid(git))

    lat = _mm(ub, wlat_ref[...])
    cos = cos_ref[...]
    sin = sin_ref[...]
    kr_ref[...] = (lat[:, Q_LORA + KV_LORA:Q_LORA + KV_LORA + LANES] * cos
                   + lat[:, Q_LORA + KV_LORA + LANES:] * sin).astype(BF16)

    cqn = _rms(lat[:, :Q_LORA], gq_ref[...]).astype(BF16)
    q3 = _mm(cqn, wuq_ref[...])
    d_n = A_HEADS * A_NOPE
    qn_ref[...] = (q3[:, :d_n] * ATT_SCALE).astype(BF16)
    for h in range(A_HEADS):
        a = q3[:, d_n + h * LANES:d_n + (h + 1) * LANES]
        b = q3[:, d_n + (A_HEADS + h) * LANES:d_n + (A_HEADS + h + 1) * LANES]
        qr_ref[:, h * LANES:(h + 1) * LANES] = ((a * cos + b * sin) * ATT_SCALE).astype(BF16)

    ckvn = _rms(lat[:, Q_LORA:Q_LORA + KV_LORA], gkv_ref[...]).astype(BF16)
    kv = _mm(ckvn, wukv_ref[...])
    kn_ref[...] = kv[:, :d_n].astype(BF16)
    va_ref[...] = kv[:, d_n:].astype(BF16)


def _in_proj(x2d, cos4, sin4, weights, tm):
    rows, d = x2d.shape
    n_tab = cos4.shape[0] // tm
    row_blk = lambda w: pl.BlockSpec((tm, w), lambda i: (i, 0))
    tab_blk = pl.BlockSpec((tm, LANES), lambda i: (i % n_tab, 0))
    d_qk, d_v, d_n = M_HEADS * M_DQK, M_HEADS * M_DV, A_HEADS * A_NOPE
    out_shape = [
        jax.ShapeDtypeStruct((rows, d_qk), BF16),
        jax.ShapeDtypeStruct((rows, d_qk), BF16),
        jax.ShapeDtypeStruct((rows, d_v), BF16),
        jax.ShapeDtypeStruct((rows, d_v), BF16),
        jax.ShapeDtypeStruct((rows, 2 * d), BF16),
        jax.ShapeDtypeStruct((rows, N_GATE_COLS), F32),
        jax.ShapeDtypeStruct((N_GATE_COLS, rows), F32),
        jax.ShapeDtypeStruct((rows, d_n), BF16),
        jax.ShapeDtypeStruct((rows, A_HEADS * LANES), BF16),
        jax.ShapeDtypeStruct((rows, d_n), BF16),
        jax.ShapeDtypeStruct((rows, A_HEADS * A_DV), BF16),
        jax.ShapeDtypeStruct((rows, LANES), BF16),
    ]
    out_specs = [row_blk(d_qk), row_blk(d_qk), row_blk(d_v), row_blk(d_v), row_blk(2 * d),
                 row_blk(N_GATE_COLS), pl.BlockSpec((N_GATE_COLS, tm), lambda i: (0, i)),
                 row_blk(d_n), row_blk(A_HEADS * LANES), row_blk(d_n), row_blk(A_HEADS * A_DV),
                 row_blk(LANES)]
    in_specs = [row_blk(d), tab_blk, tab_blk] + [_resident(w.shape) for w in weights]
    return pl.pallas_call(
        _in_proj_kernel,
        grid=(rows // tm,),
        in_specs=in_specs,
        out_specs=out_specs,
        out_shape=out_shape,
        compiler_params=pltpu.CompilerParams(dimension_semantics=("parallel",),
                                             vmem_limit_bytes=VMEM_LIMIT),
        name="in_proj",
    )(x2d, cos4, sin4, *weights)


def _mlstm_kernel(qf_ref, kf_ref, vf_ref, gf_ref, gtf_ref,
                  qb_ref, kb_ref, vb_ref, gb_ref, gtb_ref,
                  km_ref, vm_ref, gm_ref, gtm_ref,
                  hf_ref, hb_ref, c_scr, m_scr):
    c_idx = pl.program_id(1)
    lc = qf_ref.shape[0]
    lm = km_ref.shape[0]
    n_pairs = M_HEADS // 2
    half = LANES // 2

    def tri(n, lower):
        r = lax.broadcasted_iota(jnp.int32, (n, n), 0)
        c = lax.broadcasted_iota(jnp.int32, (n, n), 1)
        return (r >= c) if lower else (r <= c)

    lane = lax.broadcasted_iota(jnp.int32, (1, LANES), 1)
    row128 = lax.broadcasted_iota(jnp.int32, (LANES, 1), 0)

    def head_mask(h):
        return (lane < half) if h % 2 == 0 else (lane >= half)

    def state_update(direction, pair, k2, v_pair, cum_col, gl, tot_row):
        ones = jnp.ones(v_pair[0].shape, BF16)
        d_c = None
        decays = []
        for sub in range(2):
            h = 2 * pair + sub
            j = direction * M_HEADS + h
            jf = 2 * M_HEADS + j
            m_prev = m_scr[j][0:1, 0:1]
            tot = tot_row[:, jf:jf + 1]
            g_col = tot - cum_col[:, jf:jf + 1] + gl[:, j:j + 1]
            m_new = jnp.maximum(tot + m_prev, jnp.max(g_col, axis=0, keepdims=True))
            decays.append(jnp.exp(tot + m_prev - m_new))
            wk = jnp.exp(g_col - m_new)
            km = jnp.where(head_mask(h), k2, jnp.zeros_like(k2))
            kw = (km.astype(F32) * wk).astype(BF16)
            vext = jnp.concatenate([v_pair[sub], ones], axis=1)
            upd = _mm_tn(kw, vext)
            d_c = upd if d_c is None else d_c + upd
            m_scr[j] = jnp.broadcast_to(m_new, m_scr.shape[1:])
        decay_rows = jnp.where(row128 < half, decays[0], decays[1])
        ci = direction * n_pairs + pair
        c_scr[ci] = decay_rows * c_scr[ci] + d_c

    @pl.when(c_idx == 0)
    def _init():
        c_scr[...] = jnp.zeros_like(c_scr)
        m_scr[...] = jnp.zeros_like(m_scr)
        glm = gm_ref[...]
        cum_m = _mm_exact(tri(lm, True).astype(F32), glm)
        tot_m = cum_m[lm - 1:lm, :]
        for pair in range(n_pairs):
            k2 = km_ref[:, pair * LANES:(pair + 1) * LANES]
            v_pair = [vm_ref[:, (2 * pair + s) * M_DV:(2 * pair + s + 1) * M_DV] for s in range(2)]
            state_update(0, pair, k2, v_pair, cum_m, glm, tot_m)

    for direction, (q_ref, k_ref, v_ref, g_ref, gt_ref, h_ref) in enumerate(
            ((qf_ref, kf_ref, vf_ref, gf_ref, gtf_ref, hf_ref),
             (qb_ref, kb_ref, vb_ref, gb_ref, gtb_ref, hb_ref))):
        fwd = direction == 0
        gl = g_ref[...]
        glt = gt_ref[...]
        cum_col = _mm_exact(tri(lc, fwd).astype(F32), gl)
        cum_row = _mm_exact(glt, tri(lc, not fwd).astype(F32))
        tot_row = cum_col[lc - 1:lc, :] if fwd else cum_col[0:1, :]
        causal = tri(lc, fwd)
        for pair in range(n_pairs):
            q2 = q_ref[:, pair * LANES:(pair + 1) * LANES]
            k2 = k_ref[:, pair * LANES:(pair + 1) * LANES]
            ci = direction * n_pairs + pair
            c_b = c_scr[ci].astype(BF16)
            v_pair = []
            for sub in range(2):
                h = 2 * pair + sub
                j = direction * M_HEADS + h
                jf = 2 * M_HEADS + j
                v = v_ref[:, h * M_DV:(h + 1) * M_DV]
                v_pair.append(v)
                vext = jnp.concatenate([v, jnp.ones(v.shape, BF16)], axis=1)
                qm = jnp.where(head_mask(h), q2, jnp.zeros_like(q2))
                s = _mm_nt(qm, k2)
                b_col = cum_col[:, jf:jf + 1]
                dmat = b_col - cum_row[jf:jf + 1, :] + glt[j:j + 1, :]
                dmat = jnp.where(causal, dmat, NEG)
                m_prev = m_scr[j][0:1, 0:1]
                m_inter = b_col + m_prev
                m_t = jnp.maximum(m_inter, jnp.max(dmat, axis=-1, keepdims=True))
                w = (jnp.exp(dmat - m_t) * s).astype(BF16)
                acc = jnp.exp(m_inter - m_t) * _mm(qm, c_b) + _mm(w, vext)
                den = jnp.maximum(jnp.abs(acc[:, M_DV:]), jnp.exp(-m_t))
                h_ref[:, h * M_DV:(h + 1) * M_DV] = (acc[:, :M_DV] / den).astype(h_ref.dtype)
            state_update(direction, pair, k2, v_pair, cum_col, gl, tot_row)


def _mlstm(mq, mk, mv, gl, glt, meta, batch, lc):
    rows = mq.shape[0]
    seq = rows // batch
    nc = seq // lc
    km, vm, gm, gtm = meta
    d_qk, d_v = M_HEADS * M_DQK, M_HEADS * M_DV

    fwd_row = lambda b, c: (b * nc + c, 0)
    bwd_row = lambda b, c: (b * nc + nc - 1 - c, 0)
    fwd_col = lambda b, c: (0, b * nc + c)
    bwd_col = lambda b, c: (0, b * nc + nc - 1 - c)

    def views(row_map, col_map):
        return [pl.BlockSpec((lc, d_qk), row_map), pl.BlockSpec((lc, d_qk), row_map),
                pl.BlockSpec((lc, d_v), row_map), pl.BlockSpec((lc, N_GATE_COLS), row_map),
                pl.BlockSpec((N_GATE_COLS, lc), col_map)]

    const = lambda a: pl.BlockSpec(a.shape, lambda b, c: (0, 0))
    in_specs = views(fwd_row, fwd_col) + views(bwd_row, bwd_col) + [const(a) for a in meta]
    return pl.pallas_call(
        _mlstm_kernel,
        grid=(batch, nc),
        in_specs=in_specs,
        out_specs=[pl.BlockSpec((lc, d_v), fwd_row), pl.BlockSpec((lc, d_v), bwd_row)],
        out_shape=[jax.ShapeDtypeStruct((rows, d_v), BF16)] * 2,
        scratch_shapes=[pltpu.VMEM((2 * (M_HEADS // 2), LANES, 2 * M_DV), F32),
                        pltpu.VMEM((2 * M_HEADS, 8, LANES), F32)],
        compiler_params=pltpu.CompilerParams(dimension_semantics=("parallel", "arbitrary"),
                                             vmem_limit_bytes=VMEM_LIMIT),
        name="mlstm",
    )(mq, mk, mv, gl, glt, mq, mk, mv, gl, glt, km, vm, gm, gtm)


def _attn_kernel(qn_ref, qr_ref, kn_ref, kr_ref, v_ref, knm_ref, krm_ref, vm_ref, bias_ref,
                 o_ref, m_scr, l_scr, acc_scr):
    ki = pl.program_id(3)
    q = jnp.concatenate([qn_ref[...], qr_ref[...]], axis=1)

    @pl.when(ki == 0)
    def _meta():
        km = jnp.concatenate([knm_ref[...], krm_ref[...]], axis=1)
        s = _mm_nt(q, km) + bias_ref[...]
        m = jnp.max(s, axis=-1, keepdims=True)
        p = jnp.exp(s - m)
        m_scr[...] = jnp.broadcast_to(m, m_scr.shape)
        l_scr[...] = jnp.broadcast_to(jnp.sum(p, axis=-1, keepdims=True), l_scr.shape)
        acc_scr[...] = _mm(p.astype(BF16), vm_ref[...])

    k = jnp.concatenate([kn_ref[...], kr_ref[...]], axis=1)
    s = _mm_nt(q, k)
    m_prev = m_scr[...]
    m_new = jnp.maximum(m_prev, jnp.max(s, axis=-1, keepdims=True))
    alpha = jnp.exp(m_prev - m_new)
    p = jnp.exp(s - m_new[:, 0:1])
    l_scr[...] = alpha * l_scr[...] + jnp.sum(p, axis=-1, keepdims=True)
    acc_scr[...] = alpha * acc_scr[...] + _mm(p.astype(BF16), v_ref[...])
    m_scr[...] = m_new

    @pl.when(ki == pl.num_programs(3) - 1)
    def _finish():
        o_ref[...] = (acc_scr[...] / l_scr[...]).astype(o_ref.dtype)


def _attention(qn, qr, kn, kr, va, meta, batch, tq, tk):
    rows = qn.shape[0]
    seq = rows // batch
    nq, nk = seq // tq, seq // tk
    knm, krm, vam, bias = meta
    n_meta_pad = knm.shape[0]
    q_blk = pl.BlockSpec((tq, LANES), lambda b, h, qi, ki: (b * nq + qi, h))
    k_blk = pl.BlockSpec((tk, LANES), lambda b, h, qi, ki: (b * nk + ki, h))
    kr_blk = pl.BlockSpec((tk, LANES), lambda b, h, qi, ki: (b * nk + ki, 0))
    meta_h = pl.BlockSpec((n_meta_pad, LANES), lambda b, h, qi, ki: (0, h))
    meta_0 = pl.BlockSpec((n_meta_pad, LANES), lambda b, h, qi, ki: (0, 0))
    bias_blk = pl.BlockSpec((1, n_meta_pad), lambda b, h, qi, ki: (0, 0))
    return pl.pallas_call(
        _attn_kernel,
        grid=(batch, A_HEADS, nq, nk),
        in_specs=[q_blk, q_blk, k_blk, kr_blk, k_blk, meta_h, meta_0, meta_h, bias_blk],
        out_specs=q_blk,
        out_shape=jax.ShapeDtypeStruct((rows, A_HEADS * A_DV), BF16),
        scratch_shapes=[pltpu.VMEM((tq, LANES), F32), pltpu.VMEM((tq, LANES), F32),
                        pltpu.VMEM((tq, A_DV), F32)],
        compiler_params=pltpu.CompilerParams(
            dimension_semantics=("parallel", "parallel", "parallel", "arbitrary"),
            vmem_limit_bytes=VMEM_LIMIT),
        name="attention",
    )(qn, qr, kn, kr, va, knm, krm, vam, bias)


def _mix_kernel(x_ref, hf_ref, hb_ref, so_ref, att_ref, sg_ref, gm_ref, wm_ref, wa_ref, wo_ref,
                gffn_ref, wr_ref, br_ref, h1_ref, u2_ref, lg_ref):
    d = x_ref.shape[1]
    hs = hf_ref[...].astype(F32) + hb_ref[...].astype(F32)
    parts = []
    for h in range(M_HEADS):
        blk = hs[:, h * M_DV:(h + 1) * M_DV]
        parts.append(blk * lax.rsqrt(jnp.mean(blk * blk, axis=-1, keepdims=True) + EPS))
    hn = jnp.concatenate(parts, axis=1) * gm_ref[...] * so_ref[...].astype(F32)
    y_m = _mm(hn.astype(BF16), wm_ref[...])
    y_a = _mm(att_ref[...], wa_ref[...])
    sg = sg_ref[...].astype(F32)
    mixed = sg[:, :d] * y_m + sg[:, d:] * y_a
    h1 = x_ref[...] + _mm(mixed.astype(BF16), wo_ref[...])
    h1_ref[...] = h1
    u2 = _rms(h1, gffn_ref[...])
    u2_ref[...] = u2.astype(BF16)
    lg_ref[...] = _mm_exact(u2, wr_ref[...]) + br_ref[...]


def _mix(x2d, hf, hb, so, att, sg, weights, tm):
    rows, d = x2d.shape
    row_blk = lambda w: pl.BlockSpec((tm, w), lambda i: (i, 0))
    in_specs = ([row_blk(d), row_blk(d), row_blk(d), row_blk(d), row_blk(d), row_blk(2 * d)]
                + [_resident(w.shape) for w in weights])
    return pl.pallas_call(
        _mix_kernel,
        grid=(rows // tm,),
        in_specs=in_specs,
        out_specs=[row_blk(d), row_blk(d), row_blk(N_EXPERTS)],
        out_shape=[jax.ShapeDtypeStruct((rows, d), F32), jax.ShapeDtypeStruct((rows, d), BF16),
                   jax.ShapeDtypeStruct((rows, N_EXPERTS), F32)],
        compiler_params=pltpu.CompilerParams(dimension_semantics=("parallel",),
                                             vmem_limit_bytes=VMEM_LIMIT),
        name="mix",
    )(x2d, hf, hb, so, att, sg, *weights)


def _expert_kernel(be_ref, nused_ref, x_ref, wt_ref, w1_ref, b1_ref, w2_ref, b2_ref, y_ref):
    @pl.when(pl.program_id(0) < nused_ref[0])
    def _():
        hid = _mm(x_ref[...], w1_ref[...]) + b1_ref[...]
        glu = jnp.minimum(hid[:, :D_FF], SWIGLU_LIMIT)
        lin = jnp.clip(hid[:, D_FF:], -SWIGLU_LIMIT, SWIGLU_LIMIT)
        act = glu * _sigmoid(SWIGLU_ALPHA * glu) * (lin + 1.0)
        y_ref[...] = (_mm(act.astype(BF16), w2_ref[...]) + b2_ref[...]) * wt_ref[...]


def _experts(block_e, n_used, xg, wts, w1, b1, w2, b2, tme):
    rows, d = xg.shape
    nb = rows // tme
    grid_spec = pltpu.PrefetchScalarGridSpec(
        num_scalar_prefetch=2,
        grid=(nb,),
        in_specs=[
            pl.BlockSpec((tme, d), lambda i, be, nu: (i, 0)),
            pl.BlockSpec((tme, 1), lambda i, be, nu: (i, 0)),
            pl.BlockSpec((None, d, 2 * D_FF), lambda i, be, nu: (be[i], 0, 0)),
            pl.BlockSpec((None, 1, 2 * D_FF), lambda i, be, nu: (be[i], 0, 0)),
            pl.BlockSpec((None, D_FF, d), lambda i, be, nu: (be[i], 0, 0)),
            pl.BlockSpec((None, 1, d), lambda i, be, nu: (be[i], 0, 0)),
        ],
        out_specs=pl.BlockSpec((tme, d), lambda i, be, nu: (i, 0)),
    )
    return pl.pallas_call(
        _expert_kernel,
        grid_spec=grid_spec,
        out_shape=jax.ShapeDtypeStruct((rows, d), F32),
        compiler_params=pltpu.CompilerParams(dimension_semantics=("arbitrary",),
                                             vmem_limit_bytes=VMEM_LIMIT),
        name="experts",
    )(block_e, n_used, xg, wts, w1, b1, w2, b2)


def _final_kernel(h1_ref, y_ref, g_ref, o_ref):
    o_ref[...] = _rms(h1_ref[...] + y_ref[...], g_ref[...])


def _final(h1, y, g, tm):
    rows, d = h1.shape
    row_blk = pl.BlockSpec((tm, d), lambda i: (i, 0))
    return pl.pallas_call(
        _final_kernel,
        grid=(rows // tm,),
        in_specs=[row_blk, row_blk, pl.BlockSpec((1, d), lambda i: (0, 0))],
        out_specs=row_blk,
        out_shape=jax.ShapeDtypeStruct((rows, d), F32),
        compiler_params=pltpu.CompilerParams(dimension_semantics=("parallel",)),
        name="final",
    )(h1, y, g)


def _rope_tables(pos):
    half = A_ROPE // 2
    freqs = ROPE_THETA ** (-jnp.arange(half, dtype=F32) / half)
    ang = pos.astype(F32)[:, None] * freqs[None, :]
    zeros = jnp.zeros((pos.shape[0], LANES - A_ROPE), F32)
    cos, sin = jnp.cos(ang), jnp.sin(ang)
    return jnp.concatenate([cos, cos, zeros], axis=1), jnp.concatenate([sin, sin, zeros], axis=1)


def _rope_pair(w):
    half = A_ROPE // 2
    zeros = jnp.zeros(w.shape[:-1] + (LANES - A_ROPE,), w.dtype)
    a = jnp.concatenate([w, zeros], axis=-1)
    b = jnp.concatenate([-w[..., half:], w[..., :half], zeros], axis=-1)
    return a, b


def _pick_tile(n, prefs):
    for t in prefs:
        if n % t == 0:
            return t
    return n


def kernel(x, meta_tokens, norm_mix, w_in, m_b_i, m_b_f, m_norm, w_m_proj, a_w_uq, a_w_ukv,
           a_norm_q, a_norm_kv, w_a_proj, w_out, norm_ffn, w_router, b_router, w1, b1, w2, b2,
           norm_final):
    assert norm_mix.shape[0] == 1, "single-layer encoder"
    batch, seq, d = x.shape
    rows = batch * seq
    x2d = x.reshape(rows, d)

    sizes = (M_HEADS * M_DQK, M_HEADS * M_DQK, M_HEADS * M_DV, M_HEADS * M_DV, 2 * M_HEADS,
             2 * M_HEADS, Q_LORA, KV_LORA, A_ROPE, 2 * d)
    offs = np.cumsum((0,) + sizes)
    w_q, w_k, w_v, w_o, w_i, w_f, w_cq, w_ckv, w_kr, w_g = (
        w_in[0][:, offs[i]:offs[i + 1]] for i in range(len(sizes)))
    kr_a, kr_b = _rope_pair(w_kr)
    w_qk = jnp.concatenate([w_q * (M_DQK ** -0.5), w_k], axis=1).astype(BF16)
    w_vo = jnp.concatenate([w_v, w_o], axis=1).astype(BF16)
    w_lat = jnp.concatenate([w_cq, w_ckv, kr_a, kr_b], axis=1).astype(BF16)
    w_if = jnp.concatenate([w_i, w_f], axis=1).astype(BF16)
    b_if = jnp.concatenate([m_b_i[0].reshape(-1), m_b_f[0].reshape(-1)]).astype(F32)
    uq_a, uq_b = _rope_pair(a_w_uq[0][:, :, A_NOPE:])
    w_uq = jnp.concatenate([a_w_uq[0][:, :, :A_NOPE].reshape(Q_LORA, -1),
                            uq_a.reshape(Q_LORA, -1), uq_b.reshape(Q_LORA, -1)], axis=1).astype(BF16)
    w_ukv = jnp.concatenate([a_w_ukv[0][:, :, :A_NOPE].reshape(KV_LORA, -1),
                             a_w_ukv[0][:, :, A_NOPE:].reshape(KV_LORA, -1)], axis=1).astype(BF16)
    in_weights = (norm_mix[0].reshape(1, d), w_qk, w_vo, w_g.astype(BF16), w_lat,
                  w_if, w_if.T, b_if.reshape(1, -1), b_if.reshape(-1, 1),
                  a_norm_q[0].reshape(1, -1), a_norm_kv[0].reshape(1, -1), w_uq, w_ukv)

    cos_r, sin_r = _rope_tables(jnp.arange(N_META, N_META + seq))
    cos_m, sin_m = _rope_tables(jnp.arange(N_META))
    tm = _pick_tile(seq, (256, 128, 64, 32, 16))
    (mq, mk, mv, so, sg, gl, glt, qn, qr, kn, va, kr) = _in_proj(x2d, cos_r, sin_r, in_weights, tm)
    (_, mk_m, mv_m, _, _, gl_m, glt_m, _, _, kn_m, va_m, kr_m) = _in_proj(
        meta_tokens.astype(x.dtype), cos_m, sin_m, in_weights, N_META)

    n_pad = LANES - N_META
    pad_gate = jnp.concatenate([jnp.full((n_pad, 2 * M_HEADS), NEG, F32),
                                jnp.zeros((n_pad, 2 * M_HEADS), F32)], axis=1)
    front = lambda a: jnp.concatenate([jnp.zeros((n_pad, a.shape[1]), a.dtype), a], axis=0)
    mlstm_meta = (front(mk_m), front(mv_m), jnp.concatenate([pad_gate, gl_m], axis=0),
                  jnp.concatenate([pad_gate.T, glt_m], axis=1))
    lc = _pick_tile(seq, (256, 128))
    hf, hb = _mlstm(mq, mk, mv, gl, glt, mlstm_meta, batch, lc)

    back = lambda a: jnp.concatenate([a, jnp.zeros((n_pad, a.shape[1]), a.dtype)], axis=0)
    bias = jnp.concatenate([jnp.zeros((1, N_META), F32), jnp.full((1, n_pad), NEG, F32)], axis=1)
    tq = _pick_tile(seq, (512, 256, 128))
    tk = _pick_tile(seq, (512, 256, 128))
    att = _attention(qn, qr, kn, kr, va, (back(kn_m), back(kr_m), back(va_m), bias), batch, tq, tk)

    mix_weights = (m_norm[0].reshape(1, -1), w_m_proj[0].astype(BF16), w_a_proj[0].astype(BF16),
                   w_out[0].astype(BF16), norm_ffn[0].reshape(1, d), w_router[0],
                   b_router[0].reshape(1, -1))
    h1, u2, logits = _mix(x2d, hf, hb, so, att, sg, mix_weights, tm)

    tme = 256
    n = rows * TOP_K
    top_val, top_idx = lax.top_k(logits, TOP_K)
    gates = jax.nn.softmax(top_val, axis=-1)
    flat_e = top_idx.reshape(-1)
    order = jnp.argsort(flat_e)
    sorted_e = flat_e[order]
    counts = jnp.bincount(flat_e, length=N_EXPERTS)
    start = jnp.cumsum(counts) - counts
    padded = (counts + tme - 1) // tme * tme
    padded_end = jnp.cumsum(padded)
    padded_start = padded_end - padded
    dest = padded_start[sorted_e] + jnp.arange(n, dtype=jnp.int32) - start[sorted_e]
    nb = -(-n // tme) + N_EXPERTS
    slot = jnp.zeros((n,), jnp.int32).at[order].set(dest.astype(jnp.int32))
    buf_tok = jnp.full((nb * tme,), rows, jnp.int32).at[dest].set((order // TOP_K).astype(jnp.int32))
    buf_w = jnp.zeros((nb * tme,), F32).at[slot].set(gates.reshape(-1))
    block_e = jnp.minimum(jnp.searchsorted(padded_end, jnp.arange(nb, dtype=jnp.int32) * tme,
                                           side='right'), N_EXPERTS - 1).astype(jnp.int32)
    n_used = (padded_end[-1] // tme).astype(jnp.int32).reshape(1)
    u2_pad = jnp.concatenate([u2, jnp.zeros((1, d), u2.dtype)], axis=0)
    xg = u2_pad[buf_tok]
    yb = _experts(block_e, n_used, xg, buf_w.reshape(-1, 1), w1[0].astype(BF16),
                  b1[0].reshape(N_EXPERTS, 1, -1), w2[0].astype(BF16), b2[0].reshape(N_EXPERTS, 1, -1),
                  tme)
    y = yb[slot].reshape(rows, TOP_K, d).sum(axis=1)

    out = _final(h1, y, norm_final.reshape(1, d), tm)
    return out.reshape(batch, seq, d)
```

```python
import functools

import jax
import jax.numpy as jnp
import numpy as np
from jax import lax
from jax.experimental import pallas as pl
from jax.experimental.pallas import tpu as pltpu

N_META = 16
EPS = 1e-6
NEG = -1e30
M_HEADS = 8
M_DQK = 64
M_DV = 128
A_HEADS = 8
A_NOPE = 128
A_ROPE = 64
A_DV = 128
Q_LORA = 256
KV_LORA = 128
ROPE_THETA = 10000.0
N_EXPERTS = 32
TOP_K = 4
D_FF = 1024
SWIGLU_ALPHA = 1.702
SWIGLU_LIMIT = 7.0

LANES = 128
N_GATE_COLS = 4 * M_HEADS
ATT_SCALE = (A_NOPE + A_ROPE) ** -0.5
VMEM_LIMIT = 56 * 1024 * 1024

F32 = jnp.float32
BF16 = jnp.bfloat16


def _mm(a, b):
    return jnp.dot(a, b, preferred_element_type=F32)


def _mm_nt(a, b):
    return lax.dot_general(a, b, (((1,), (1,)), ((), ())), preferred_element_type=F32)


def _mm_tn(a, b):
    return lax.dot_general(a, b, (((0,), (0,)), ((), ())), preferred_element_type=F32)


def _mm_exact(a, b):
    return jnp.dot(a, b, preferred_element_type=F32, precision=lax.Precision.HIGHEST)


def _rms(x, g):
    return x * lax.rsqrt(jnp.mean(x * x, axis=-1, keepdims=True) + EPS) * g


def _log_sigmoid(x):
    return jnp.minimum(x, 0.0) - jnp.log1p(jnp.exp(-jnp.abs(x)))


def _sigmoid(x):
    return 1.0 / (1.0 + jnp.exp(-x))


def _resident(shape):
    nd = len(shape)
    return pl.BlockSpec(shape, lambda *_: (0,) * nd, pipeline_mode=pl.Buffered(1))


def _in_proj_kernel(x_ref, cos_ref, sin_ref, gmix_ref, wqk_ref, wvo_ref, wgate_ref, wlat_ref,
                    wif_ref, wift_ref, bif_ref, bift_ref, gq_ref, gkv_ref, wuq_ref, wukv_ref,
                    mq_ref, mk_ref, mv_ref, so_ref, sg_ref, gl_ref, glt_ref,
                    qn_ref, qr_ref, kn_ref, va_ref, kr_ref):
    d_qk = M_HEADS * M_DQK
    d_v = M_HEADS * M_DV
    ub = _rms(x_ref[...], gmix_ref[...]).astype(BF16)

    qk = _mm(ub, wqk_ref[...])
    mq_ref[...] = qk[:, :d_qk].astype(BF16)
    mk_ref[...] = qk[:, d_qk:].astype(BF16)
    vo = _mm(ub, wvo_ref[...])
    mv_ref[...] = vo[:, :d_v].astype(BF16)
    so_ref[...] = _sigmoid(vo[:, d_v:]).astype(BF16)
    sg_ref[...] = _sigmoid(_mm(ub, wgate_ref[...])).astype(BF16)

    gi = _mm(ub, wif_ref[...]) + bif_ref[...]
    col = lax.broadcasted_iota(jnp.int32, gi.shape, 1)
    gl_ref[...] = jnp.where(col < 2 * M_HEADS, gi, _log_sigmoid(gi))
    git = _mm_nt(wift_ref[...], ub) + bift_ref[...]
    row = lax.broadcasted_iota(jnp.int32, git.shape, 0)
    glt_ref[...] = jnp.where(row < 2 * M_HEADS, git, _log_sigmoid(git))

    lat = _mm(ub, wlat_ref[...])
    cos = cos_ref[...]
    sin = sin_ref[...]
    kr_ref[...] = (lat[:, Q_LORA + KV_LORA:Q_LORA + KV_LORA + LANES] * cos
                   + lat[:, Q_LORA + KV_LORA + LANES:] * sin).astype(BF16)

    cqn = _rms(lat[:, :Q_LORA], gq_ref[...]).astype(BF16)
    q3 = _mm(cqn, wuq_ref[...])
    d_n = A_HEADS * A_NOPE
    qn_ref[...] = (q3[:, :d_n] * ATT_SCALE).astype(BF16)
    for h in range(A_HEADS):
        a = q3[:, d_n + h * LANES:d_n + (h + 1) * LANES]
        b = q3[:, d_n + (A_HEADS + h) * LANES:d_n + (A_HEADS + h + 1) * LANES]
        qr_ref[:, h * LANES:(h + 1) * LANES] = ((a * cos + b * sin) * ATT_SCALE).astype(BF16)

    ckvn = _rms(lat[:, Q_LORA:Q_LORA + KV_LORA], gkv_ref[...]).astype(BF16)
    kv = _mm(ckvn, wukv_ref[...])
    kn_ref[...] = kv[:, :d_n].astype(BF16)
    va_ref[...] = kv[:, d_n:].astype(BF16)


def _in_proj(x2d, cos4, sin4, weights, tm):
    rows, d = x2d.shape
    n_tab = cos4.shape[0] // tm
    row_blk = lambda w: pl.BlockSpec((tm, w), lambda i: (i, 0))
    tab_blk = pl.BlockSpec((tm, LANES), lambda i: (i % n_tab, 0))
    d_qk, d_v, d_n = M_HEADS * M_DQK, M_HEADS * M_DV, A_HEADS * A_NOPE
    out_shape = [
        jax.ShapeDtypeStruct((rows, d_qk), BF16),
        jax.ShapeDtypeStruct((rows, d_qk), BF16),
        jax.ShapeDtypeStruct((rows, d_v), BF16),
        jax.ShapeDtypeStruct((rows, d_v), BF16),
        jax.ShapeDtypeStruct((rows, 2 * d), BF16),
        jax.ShapeDtypeStruct((rows, N_GATE_COLS), F32),
        jax.ShapeDtypeStruct((N_GATE_COLS, rows), F32),
        jax.ShapeDtypeStruct((rows, d_n), BF16),
        jax.ShapeDtypeStruct((rows, A_HEADS * LANES), BF16),
        jax.ShapeDtypeStruct((rows, d_n), BF16),
        jax.ShapeDtypeStruct((rows, A_HEADS * A_DV), BF16),
        jax.ShapeDtypeStruct((rows, LANES), BF16),
    ]
    out_specs = [row_blk(d_qk), row_blk(d_qk), row_blk(d_v), row_blk(d_v), row_blk(2 * d),
                 row_blk(N_GATE_COLS), pl.BlockSpec((N_GATE_COLS, tm), lambda i: (0, i)),
                 row_blk(d_n), row_blk(A_HEADS * LANES), row_blk(d_n), row_blk(A_HEADS * A_DV),
                 row_blk(LANES)]
    in_specs = [row_blk(d), tab_blk, tab_blk] + [_resident(w.shape) for w in weights]
    return pl.pallas_call(
        _in_proj_kernel,
        grid=(rows // tm,),
        in_specs=in_specs,
        out_specs=out_specs,
        out_shape=out_shape,
        compiler_params=pltpu.CompilerParams(dimension_semantics=("parallel",),
                                             vmem_limit_bytes=VMEM_LIMIT),
        name="in_proj",
    )(x2d, cos4, sin4, *weights)


def _mlstm_kernel(qf_ref, kf_ref, vf_ref, gf_ref, gtf_ref,
                  qb_ref, kb_ref, vb_ref, gb_ref, gtb_ref,
                  km_ref, vm_ref, gm_ref, gtm_ref,
                  hf_ref, hb_ref, c_scr, m_scr):
    c_idx = pl.program_id(1)
    lc = qf_ref.shape[0]
    lm = km_ref.shape[0]
    n_pairs = M_HEADS // 2
    half = LANES // 2

    def tri(n, lower):
        r = lax.broadcasted_iota(jnp.int32, (n, n), 0)
        c = lax.broadcasted_iota(jnp.int32, (n, n), 1)
        return (r >= c) if lower else (r <= c)

    lane = lax.broadcasted_iota(jnp.int32, (1, LANES), 1)
    row128 = lax.broadcasted_iota(jnp.int32, (LANES, 1), 0)

    def head_mask(h):
        return (lane < half) if h % 2 == 0 else (lane >= half)

    def state_update(direction, pair, k2, v_pair, cum_col, gl, tot_row):
        ones = jnp.ones(v_pair[0].shape, BF16)
        d_c = None
        decays = []
        for sub in range(2):
            h = 2 * pair + sub
            j = direction * M_HEADS + h
            jf = 2 * M_HEADS + j
            m_prev = m_scr[j][0:1, 0:1]
            tot = tot_row[:, jf:jf + 1]
            g_col = tot - cum_col[:, jf:jf + 1] + gl[:, j:j + 1]
            m_new = jnp.maximum(tot + m_prev, jnp.max(g_col, axis=0, keepdims=True))
            decays.append(jnp.exp(tot + m_prev - m_new))
            wk = jnp.exp(g_col - m_new)
            km = jnp.where(head_mask(h), k2, jnp.zeros_like(k2))
            kw = (km.astype(F32) * wk).astype(BF16)
            vext = jnp.concatenate([v_pair[sub], ones], axis=1)
            upd = _mm_tn(kw, vext)
            d_c = upd if d_c is None else d_c + upd
            m_scr[j] = jnp.broadcast_to(m_new, m_scr.shape[1:])
        decay_rows = jnp.where(row128 < half, decays[0], decays[1])
        ci = direction * n_pairs + pair
        c_scr[ci] = decay_rows * c_scr[ci] + d_c

    @pl.when(c_idx == 0)
    def _init():
        c_scr[...] = jnp.zeros_like(c_scr)
        m_scr[...] = jnp.zeros_like(m_scr)
        glm = gm_ref[...]
        cum_m = _mm_exact(tri(lm, True).astype(F32), glm)
        tot_m = cum_m[lm - 1:lm, :]
        for pair in range(n_pairs):
            k2 = km_ref[:, pair * LANES:(pair + 1) * LANES]
            v_pair = [vm_ref[:, (2 * pair + s) * M_DV:(2 * pair + s + 1) * M_DV] for s in range(2)]
            state_update(0, pair, k2, v_pair, cum_m, glm, tot_m)

    for direction, (q_ref, k_ref, v_ref, g_ref, gt_ref, h_ref) in enumerate(
            ((qf_ref, kf_ref, vf_ref, gf_ref, gtf_ref, hf_ref),
             (qb_ref, kb_ref, vb_ref, gb_ref, gtb_ref, hb_ref))):
        fwd = direction == 0
        gl = g_ref[...]
        glt = gt_ref[...]
        cum_col = _mm_exact(tri(lc, fwd).astype(F32), gl)
        cum_row = _mm_exact(glt, tri(lc, not fwd).astype(F32))
        tot_row = cum_col[lc - 1:lc, :] if fwd else cum_col[0:1, :]
        causal = tri(lc, fwd)
        for pair in range(n_pairs):
            q2 = q_ref[:, pair * LANES:(pair + 1) * LANES]
            k2 = k_ref[:, pair * LANES:(pair + 1) * LANES]
            ci = direction * n_pairs + pair
            c_b = c_scr[ci].astype(BF16)
            v_pair = []
            for sub in range(2):
                h = 2 * pair + sub
                j = direction * M_HEADS + h
                jf = 2 * M_HEADS + j
                v = v_ref[:, h * M_DV:(h + 1) * M_DV]
                v_pair.append(v)
                vext = jnp.concatenate([v, jnp.ones(v.shape, BF16)], axis=1)
                qm = jnp.where(head_mask(h), q2, jnp.zeros_like(q2))
                s = _mm_nt(qm, k2)
                b_col = cum_col[:, jf:jf + 1]
                dmat = b_col - cum_row[jf:jf + 1, :] + glt[j:j + 1, :]
                dmat = jnp.where(causal, dmat, NEG)
                m_prev = m_scr[j][0:1, 0:1]
                m_inter = b_col + m_prev
                m_t = jnp.maximum(m_inter, jnp.max(dmat, axis=-1, keepdims=True))
                w = (jnp.exp(dmat - m_t) * s).astype(BF16)
                acc = jnp.exp(m_inter - m_t) * _mm(qm, c_b) + _mm(w, vext)
                den = jnp.maximum(jnp.abs(acc[:, M_DV:]), jnp.exp(-m_t))
                h_ref[:, h * M_DV:(h + 1) * M_DV] = (acc[:, :M_DV] / den).astype(h_ref.dtype)
            state_update(direction, pair, k2, v_pair, cum_col, gl, tot_row)


def _mlstm(mq, mk, mv, gl, glt, meta, batch, lc):
    rows = mq.shape[0]
    seq = rows // batch
    nc = seq // lc
    km, vm, gm, gtm = meta
    d_qk, d_v = M_HEADS * M_DQK, M_HEADS * M_DV

    fwd_row = lambda b, c: (b * nc + c, 0)
    bwd_row = lambda b, c: (b * nc + nc - 1 - c, 0)
    fwd_col = lambda b, c: (0, b * nc + c)
    bwd_col = lambda b, c: (0, b * nc + nc - 1 - c)

    def views(row_map, col_map):
        return [pl.BlockSpec((lc, d_qk), row_map), pl.BlockSpec((lc, d_qk), row_map),
                pl.BlockSpec((lc, d_v), row_map), pl.BlockSpec((lc, N_GATE_COLS), row_map),
                pl.BlockSpec((N_GATE_COLS, lc), col_map)]

    const = lambda a: pl.BlockSpec(a.shape, lambda b, c: (0, 0))
    in_specs = views(fwd_row, fwd_col) + views(bwd_row, bwd_col) + [const(a) for a in meta]
    return pl.pallas_call(
        _mlstm_kernel,
        grid=(batch, nc),
        in_specs=in_specs,
        out_specs=[pl.BlockSpec((lc, d_v), fwd_row), pl.BlockSpec((lc, d_v), bwd_row)],
        out_shape=[jax.ShapeDtypeStruct((rows, d_v), BF16)] * 2,
        scratch_shapes=[pltpu.VMEM((2 * (M_HEADS // 2), LANES, 2 * M_DV), F32),
                        pltpu.VMEM((2 * M_HEADS, 8, LANES), F32)],
        compiler_params=pltpu.CompilerParams(dimension_semantics=("parallel", "arbitrary"),
                                             vmem_limit_bytes=VMEM_LIMIT),
        name="mlstm",
    )(mq, mk, mv, gl, glt, mq, mk, mv, gl, glt, km, vm, gm, gtm)


def _attn_kernel(qn_ref, qr_ref, kn_ref, kr_ref, v_ref, knm_ref, krm_ref, vm_ref, bias_ref,
                 o_ref, *, tk):
    q = jnp.concatenate([qn_ref[...], qr_ref[...]], axis=1)

    def vext(v):
        return jnp.concatenate([v, jnp.ones(v.shape, BF16)], axis=1)

    km = jnp.concatenate([knm_ref[...], krm_ref[...]], axis=1)
    s = _mm_nt(q, km) + bias_ref[...]
    m = jnp.max(s, axis=-1, keepdims=True)
    acc = _mm(jnp.exp(s - m).astype(BF16), vext(vm_ref[...]))
    for j in range(kn_ref.shape[0] // tk):
        rows = slice(j * tk, (j + 1) * tk)
        k = jnp.concatenate([kn_ref[rows, :], kr_ref[rows, :]], axis=1)
        s = _mm_nt(q, k)
        m_new = jnp.maximum(m, jnp.max(s, axis=-1, keepdims=True))
        p = jnp.exp(s - m_new).astype(BF16)
        acc = jnp.exp(m - m_new) * acc + _mm(p, vext(v_ref[rows, :]))
        m = m_new
    o_ref[...] = (acc[:, :A_DV] / acc[:, A_DV:]).astype(o_ref.dtype)


def _attention(qn, qr, kn, kr, va, meta, batch, tq, tk):
    rows = qn.shape[0]
    seq = rows // batch
    nq = seq // tq
    knm, krm, vam, bias = meta
    n_meta_pad = knm.shape[0]
    q_blk = pl.BlockSpec((tq, LANES), lambda b, h, qi: (b * nq + qi, h))
    k_blk = pl.BlockSpec((seq, LANES), lambda b, h, qi: (b, h))
    kr_blk = pl.BlockSpec((seq, LANES), lambda b, h, qi: (b, 0))
    meta_h = pl.BlockSpec((n_meta_pad, LANES), lambda b, h, qi: (0, h))
    meta_0 = pl.BlockSpec((n_meta_pad, LANES), lambda b, h, qi: (0, 0))
    bias_blk = pl.BlockSpec((1, n_meta_pad), lambda b, h, qi: (0, 0))
    return pl.pallas_call(
        functools.partial(_attn_kernel, tk=tk),
        grid=(batch, A_HEADS, nq),
        in_specs=[q_blk, q_blk, k_blk, kr_blk, k_blk, meta_h, meta_0, meta_h, bias_blk],
        out_specs=q_blk,
        out_shape=jax.ShapeDtypeStruct((rows, A_HEADS * A_DV), BF16),
        compiler_params=pltpu.CompilerParams(
            dimension_semantics=("parallel", "parallel", "parallel"),
            vmem_limit_bytes=VMEM_LIMIT),
        name="attention",
    )(qn, qr, kn, kr, va, knm, krm, vam, bias)


def _mix_kernel(x_ref, hf_ref, hb_ref, so_ref, att_ref, sg_ref, gm_ref, wm_ref, wa_ref, wo_ref,
                gffn_ref, wr_ref, br_ref, h1_ref, u2_ref, lg_ref):
    d = x_ref.shape[1]
    hs = hf_ref[...].astype(F32) + hb_ref[...].astype(F32)
    parts = []
    for h in range(M_HEADS):
        blk = hs[:, h * M_DV:(h + 1) * M_DV]
        parts.append(blk * lax.rsqrt(jnp.mean(blk * blk, axis=-1, keepdims=True) + EPS))
    hn = jnp.concatenate(parts, axis=1) * gm_ref[...] * so_ref[...].astype(F32)
    y_m = _mm(hn.astype(BF16), wm_ref[...])
    y_a = _mm(att_ref[...], wa_ref[...])
    sg = sg_ref[...].astype(F32)
    mixed = sg[:, :d] * y_m + sg[:, d:] * y_a
    h1 = x_ref[...] + _mm(mixed.astype(BF16), wo_ref[...])
    h1_ref[...] = h1
    u2 = _rms(h1, gffn_ref[...])
    u2_ref[...] = u2.astype(BF16)
    lg_ref[...] = _mm_exact(u2, wr_ref[...]) + br_ref[...]


def _mix(x2d, hf, hb, so, att, sg, weights, tm):
    rows, d = x2d.shape
    row_blk = lambda w: pl.BlockSpec((tm, w), lambda i: (i, 0))
    in_specs = ([row_blk(d), row_blk(d), row_blk(d), row_blk(d), row_blk(d), row_blk(2 * d)]
                + [_resident(w.shape) for w in weights])
    return pl.pallas_call(
        _mix_kernel,
        grid=(rows // tm,),
        in_specs=in_specs,
        out_specs=[row_blk(d), row_blk(d), row_blk(N_EXPERTS)],
        out_shape=[jax.ShapeDtypeStruct((rows, d), F32), jax.ShapeDtypeStruct((rows, d), BF16),
                   jax.ShapeDtypeStruct((rows, N_EXPERTS), F32)],
        compiler_params=pltpu.CompilerParams(dimension_semantics=("parallel",),
                                             vmem_limit_bytes=VMEM_LIMIT),
        name="mix",
    )(x2d, hf, hb, so, att, sg, *weights)


def _expert_kernel(be_ref, nused_ref, x_ref, w1_ref, b1_ref, w2_ref, b2_ref, y_ref, w1b_scr, w2b_scr):
    i = pl.program_id(0)

    @pl.when(i < nused_ref[0])
    def _():
        @pl.when((i == 0) | (be_ref[i] != be_ref[jnp.maximum(i - 1, 0)]))
        def _cast():
            w1b_scr[...] = w1_ref[...].astype(BF16)
            w2b_scr[...] = w2_ref[...].astype(BF16)

        hid = _mm(x_ref[...], w1b_scr[...]) + b1_ref[...]
        glu = jnp.minimum(hid[:, :D_FF], SWIGLU_LIMIT)
        lin = jnp.clip(hid[:, D_FF:], -SWIGLU_LIMIT, SWIGLU_LIMIT)
        act = glu * _sigmoid(SWIGLU_ALPHA * glu) * (lin + 1.0)
        y_ref[...] = (_mm(act.astype(BF16), w2b_scr[...]) + b2_ref[...]).astype(y_ref.dtype)


def _experts(block_e, n_used, xg, w1, b1, w2, b2, tme):
    rows, d = xg.shape
    nb = rows // tme
    grid_spec = pltpu.PrefetchScalarGridSpec(
        num_scalar_prefetch=2,
        grid=(nb,),
        in_specs=[
            pl.BlockSpec((tme, d), lambda i, be, nu: (i, 0)),
            pl.BlockSpec((None, d, 2 * D_FF), lambda i, be, nu: (be[i], 0, 0)),
            pl.BlockSpec((None, 1, 2 * D_FF), lambda i, be, nu: (be[i], 0, 0)),
            pl.BlockSpec((None, D_FF, d), lambda i, be, nu: (be[i], 0, 0)),
            pl.BlockSpec((None, 1, d), lambda i, be, nu: (be[i], 0, 0)),
        ],
        out_specs=pl.BlockSpec((tme, d), lambda i, be, nu: (i, 0)),
        scratch_shapes=[pltpu.VMEM((d, 2 * D_FF), BF16), pltpu.VMEM((D_FF, d), BF16)],
    )
    return pl.pallas_call(
        _expert_kernel,
        grid_spec=grid_spec,
        out_shape=jax.ShapeDtypeStruct((rows, d), BF16),
        compiler_params=pltpu.CompilerParams(dimension_semantics=("arbitrary",),
                                             vmem_limit_bytes=VMEM_LIMIT),
        name="experts",
    )(block_e, n_used, xg, w1, b1, w2, b2)


def _final_kernel(h1_ref, yg_ref, gate_ref, g_ref, o_ref):
    gates = gate_ref[...]
    y = h1_ref[...]
    for k in range(TOP_K):
        y = y + yg_ref[k].astype(F32) * gates[:, k:k + 1]
    o_ref[...] = _rms(y, g_ref[...])


def _final(h1, yg, gates, g, tm):
    rows, d = h1.shape
    row_blk = pl.BlockSpec((tm, d), lambda i: (i, 0))
    return pl.pallas_call(
        _final_kernel,
        grid=(rows // tm,),
        in_specs=[row_blk, pl.BlockSpec((TOP_K, tm, d), lambda i: (0, i, 0)),
                  pl.BlockSpec((tm, TOP_K), lambda i: (i, 0)), pl.BlockSpec((1, d), lambda i: (0, 0))],
        out_specs=row_blk,
        out_shape=jax.ShapeDtypeStruct((rows, d), F32),
        compiler_params=pltpu.CompilerParams(dimension_semantics=("parallel",)),
        name="final",
    )(h1, yg, gates, g)


def _rope_tables(pos):
    half = A_ROPE // 2
    freqs = ROPE_THETA ** (-jnp.arange(half, dtype=F32) / half)
    ang = pos.astype(F32)[:, None] * freqs[None, :]
    zeros = jnp.zeros((pos.shape[0], LANES - A_ROPE), F32)
    cos, sin = jnp.cos(ang), jnp.sin(ang)
    return jnp.concatenate([cos, cos, zeros], axis=1), jnp.concatenate([sin, sin, zeros], axis=1)


def _rope_pair(w):
    half = A_ROPE // 2
    zeros = jnp.zeros(w.shape[:-1] + (LANES - A_ROPE,), w.dtype)
    a = jnp.concatenate([w, zeros], axis=-1)
    b = jnp.concatenate([-w[..., half:], w[..., :half], zeros], axis=-1)
    return a, b


def _pick_tile(n, prefs):
    for t in prefs:
        if n % t == 0:
            return t
    return n


def kernel(x, meta_tokens, norm_mix, w_in, m_b_i, m_b_f, m_norm, w_m_proj, a_w_uq, a_w_ukv,
           a_norm_q, a_norm_kv, w_a_proj, w_out, norm_ffn, w_router, b_router, w1, b1, w2, b2,
           norm_final):
    assert norm_mix.shape[0] == 1, "single-layer encoder"
    batch, seq, d = x.shape
    rows = batch * seq
    x2d = x.reshape(rows, d)

    sizes = (M_HEADS * M_DQK, M_HEADS * M_DQK, M_HEADS * M_DV, M_HEADS * M_DV, 2 * M_HEADS,
             2 * M_HEADS, Q_LORA, KV_LORA, A_ROPE, 2 * d)
    offs = np.cumsum((0,) + sizes)
    w_q, w_k, w_v, w_o, w_i, w_f, w_cq, w_ckv, w_kr, w_g = (
        w_in[0][:, offs[i]:offs[i + 1]] for i in range(len(sizes)))
    kr_a, kr_b = _rope_pair(w_kr)
    w_qk = jnp.concatenate([w_q * (M_DQK ** -0.5), w_k], axis=1).astype(BF16)
    w_vo = jnp.concatenate([w_v, w_o], axis=1).astype(BF16)
    w_lat = jnp.concatenate([w_cq, w_ckv, kr_a, kr_b], axis=1).astype(BF16)
    w_if = jnp.concatenate([w_i, w_f], axis=1).astype(BF16)
    b_if = jnp.concatenate([m_b_i[0].reshape(-1), m_b_f[0].reshape(-1)]).astype(F32)
    uq_a, uq_b = _rope_pair(a_w_uq[0][:, :, A_NOPE:])
    w_uq = jnp.concatenate([a_w_uq[0][:, :, :A_NOPE].reshape(Q_LORA, -1),
                            uq_a.reshape(Q_LORA, -1), uq_b.reshape(Q_LORA, -1)], axis=1).astype(BF16)
    w_ukv = jnp.concatenate([a_w_ukv[0][:, :, :A_NOPE].reshape(KV_LORA, -1),
                             a_w_ukv[0][:, :, A_NOPE:].reshape(KV_LORA, -1)], axis=1).astype(BF16)
    in_weights = (norm_mix[0].reshape(1, d), w_qk, w_vo, w_g.astype(BF16), w_lat,
                  w_if, w_if.T, b_if.reshape(1, -1), b_if.reshape(-1, 1),
                  a_norm_q[0].reshape(1, -1), a_norm_kv[0].reshape(1, -1), w_uq, w_ukv)

    cos_r, sin_r = _rope_tables(jnp.arange(N_META, N_META + seq))
    cos_m, sin_m = _rope_tables(jnp.arange(N_META))
    tm = _pick_tile(seq, (256, 128, 64, 32, 16))
    (mq, mk, mv, so, sg, gl, glt, qn, qr, kn, va, kr) = _in_proj(x2d, cos_r, sin_r, in_weights, tm)
    (_, mk_m, mv_m, _, _, gl_m, glt_m, _, _, kn_m, va_m, kr_m) = _in_proj(
        meta_tokens.astype(x.dtype), cos_m, sin_m, in_weights, N_META)

    n_pad = LANES - N_META
    pad_gate = jnp.concatenate([jnp.full((n_pad, 2 * M_HEADS), NEG, F32),
                                jnp.zeros((n_pad, 2 * M_HEADS), F32)], axis=1)
    front = lambda a: jnp.concatenate([jnp.zeros((n_pad, a.shape[1]), a.dtype), a], axis=0)
    mlstm_meta = (front(mk_m), front(mv_m), jnp.concatenate([pad_gate, gl_m], axis=0),
                  jnp.concatenate([pad_gate.T, glt_m], axis=1))
    lc = _pick_tile(seq, (256, 128))
    hf, hb = _mlstm(mq, mk, mv, gl, glt, mlstm_meta, batch, lc)

    back = lambda a: jnp.concatenate([a, jnp.zeros((n_pad, a.shape[1]), a.dtype)], axis=0)
    bias = jnp.concatenate([jnp.zeros((1, N_META), F32), jnp.full((1, n_pad), NEG, F32)], axis=1)
    tq = _pick_tile(seq, (512, 256, 128))
    tk = _pick_tile(seq, (512, 256, 128))
    att = _attention(qn, qr, kn, kr, va, (back(kn_m), back(kr_m), back(va_m), bias), batch, tq, tk)

    mix_weights = (m_norm[0].reshape(1, -1), w_m_proj[0].astype(BF16), w_a_proj[0].astype(BF16),
                   w_out[0].astype(BF16), norm_ffn[0].reshape(1, d), w_router[0],
                   b_router[0].reshape(1, -1))
    h1, u2, logits = _mix(x2d, hf, hb, so, att, sg, mix_weights, tm)

    tme = 256
    n = rows * TOP_K
    nb = -(-n // tme) + N_EXPERTS
    i32 = jnp.int32
    top_val, top_idx = lax.top_k(logits, TOP_K)
    gates = jax.nn.softmax(top_val, axis=-1)
    flat_e = top_idx.reshape(-1).astype(i32)
    pair_ids = jnp.arange(n, dtype=i32)
    _, order = lax.sort((flat_e, pair_ids), num_keys=1, is_stable=True)
    _, rank = lax.sort((order, pair_ids), num_keys=1)
    counts = jnp.sum(flat_e[:, None] == jnp.arange(N_EXPERTS, dtype=i32)[None, :], axis=0, dtype=i32)
    start = jnp.cumsum(counts) - counts
    padded = (counts + tme - 1) // tme * tme
    padded_end = jnp.cumsum(padded)
    padded_start = padded_end - padded
    block_row0 = jnp.arange(nb, dtype=i32) * tme
    block_e = jnp.minimum(jnp.sum(padded_end[None, :] <= block_row0[:, None], axis=1, dtype=i32),
                          N_EXPERTS - 1)
    n_used = (padded_end[-1] // tme).astype(i32).reshape(1)
    off = (block_row0 - padded_start[block_e])[:, None] + jnp.arange(tme, dtype=i32)[None, :]
    valid = off < counts[block_e][:, None]
    src = jnp.clip(start[block_e][:, None] + off, 0, n - 1)
    buf_tok = jnp.where(valid, order[src] // TOP_K, rows).reshape(-1)
    slot = padded_start[flat_e] + rank - start[flat_e]
    u2_pad = jnp.concatenate([u2, jnp.zeros((1, d), u2.dtype)], axis=0)
    xg = u2_pad[buf_tok]
    yb = _experts(block_e, n_used, xg, w1[0], b1[0].reshape(N_EXPERTS, 1, -1), w2[0],
                  b2[0].reshape(N_EXPERTS, 1, -1), tme)
    yg = yb[slot.reshape(rows, TOP_K).T.reshape(-1)].reshape(TOP_K, rows, d)

    out = _final(h1, yg, gates, norm_final.reshape(1, d), tm)
    return out.reshape(batch, seq, d)
```

```python
import functools

import jax
import jax.numpy as jnp
import numpy as np
from jax import lax
from jax.experimental import pallas as pl
from jax.experimental.pallas import tpu as pltpu

N_META = 16
EPS = 1e-6
NEG = -1e30
M_HEADS = 8
M_DQK = 64
M_DV = 128
A_HEADS = 8
A_NOPE = 128
A_ROPE = 64
A_DV = 128
Q_LORA = 256
KV_LORA = 128
ROPE_THETA = 10000.0
N_EXPERTS = 32
TOP_K = 4
D_FF = 1024
SWIGLU_ALPHA = 1.702
SWIGLU_LIMIT = 7.0

LANES = 128
N_GATE_COLS = 4 * M_HEADS
ATT_SCALE = (A_NOPE + A_ROPE) ** -0.5
VMEM_LIMIT = 56 * 1024 * 1024

F32 = jnp.float32
BF16 = jnp.bfloat16


def _mm(a, b):
    return jnp.dot(a, b, preferred_element_type=F32)


def _mm_nt(a, b):
    return lax.dot_general(a, b, (((1,), (1,)), ((), ())), preferred_element_type=F32)


def _mm_tn(a, b):
    return lax.dot_general(a, b, (((0,), (0,)), ((), ())), preferred_element_type=F32)


def _mm_exact(a, b):
    return jnp.dot(a, b, preferred_element_type=F32, precision=lax.Precision.HIGHEST)


def _rms(x, g):
    return x * lax.rsqrt(jnp.mean(x * x, axis=-1, keepdims=True) + EPS) * g


def _log_sigmoid(x):
    return jnp.minimum(x, 0.0) - jnp.log1p(jnp.exp(-jnp.abs(x)))


def _sigmoid(x):
    return 1.0 / (1.0 + jnp.exp(-x))


def _resident(shape):
    nd = len(shape)
    return pl.BlockSpec(shape, lambda *_: (0,) * nd, pipeline_mode=pl.Buffered(1))


def _in_proj_kernel(x_ref, cos_ref, sin_ref, gmix_ref, wqk_ref, wvo_ref, wgate_ref, wlat_ref,
                    wif_ref, wift_ref, bif_ref, bift_ref, gq_ref, gkv_ref, wuq_ref, wukv_ref,
                    mq_ref, mk_ref, mv_ref, so_ref, sg_ref, gl_ref, glt_ref,
                    qn_ref, qr_ref, kn_ref, va_ref, kr_ref):
    d_qk = M_HEADS * M_DQK
    d_v = M_HEADS * M_DV
    ub = _rms(x_ref[...], gmix_ref[...]).astype(BF16)

    qk = _mm(ub, wqk_ref[...])
    mq_ref[...] = qk[:, :d_qk].astype(BF16)
    mk_ref[...] = qk[:, d_qk:].astype(BF16)
    vo = _mm(ub, wvo_ref[...])
    mv_ref[...] = vo[:, :d_v].astype(BF16)
    so_ref[...] = _sigmoid(vo[:, d_v:]).astype(BF16)
    sg_ref[...] = _sigmoid(_mm(ub, wgate_ref[...])).astype(BF16)

    gi = _mm(ub, wif_ref[...]) + bif_ref[...]
    col = lax.broadcasted_iota(jnp.int32, gi.shape, 1)
    gl_ref[...] = jnp.where(col < 2 * M_HEADS, gi, _log_sigmoid(gi))
    git = _mm_nt(wift_ref[...], ub) + bift_ref[...]
    row = lax.broadcasted_iota(jnp.int32, git.shape, 0)
    glt_ref[...] = jnp.where(row < 2 * M_HEADS, git, _log_sigmoid(git))

    lat = _mm(ub, wlat_ref[...])
    cos = cos_ref[...]
    sin = sin_ref[...]
    kr_ref[...] = (lat[:, Q_LORA + KV_LORA:Q_LORA + KV_LORA + LANES] * cos
                   + lat[:, Q_LORA + KV_LORA + LANES:] * sin).astype(BF16)

    cqn = _rms(lat[:, :Q_LORA], gq_ref[...]).astype(BF16)
    q3 = _mm(cqn, wuq_ref[...])
    d_n = A_HEADS * A_NOPE
    qn_ref[...] = (q3[:, :d_n] * ATT_SCALE).astype(BF16)
    for h in range(A_HEADS):
        a = q3[:, d_n + h * LANES:d_n + (h + 1) * LANES]
        b = q3[:, d_n + (A_HEADS + h) * LANES:d_n + (A_HEADS + h + 1) * LANES]
        qr_ref[:, h * LANES:(h + 1) * LANES] = ((a * cos + b * sin) * ATT_SCALE).astype(BF16)

    ckvn = _rms(lat[:, Q_LORA:Q_LORA + KV_LORA], gkv_ref[...]).astype(BF16)
    kv = _mm(ckvn, wukv_ref[...])
    kn_ref[...] = kv[:, :d_n].astype(BF16)
    va_ref[...] = kv[:, d_n:].astype(BF16)


def _in_proj(x2d, cos4, sin4, weights, tm):
    rows, d = x2d.shape
    n_tab = cos4.shape[0] // tm
    row_blk = lambda w: pl.BlockSpec((tm, w), lambda i: (i, 0))
    tab_blk = pl.BlockSpec((tm, LANES), lambda i: (i % n_tab, 0))
    d_qk, d_v, d_n = M_HEADS * M_DQK, M_HEADS * M_DV, A_HEADS * A_NOPE
    out_shape = [
        jax.ShapeDtypeStruct((rows, d_qk), BF16),
        jax.ShapeDtypeStruct((rows, d_qk), BF16),
        jax.ShapeDtypeStruct((rows, d_v), BF16),
        jax.ShapeDtypeStruct((rows, d_v), BF16),
        jax.ShapeDtypeStruct((rows, 2 * d), BF16),
        jax.ShapeDtypeStruct((rows, N_GATE_COLS), F32),
        jax.ShapeDtypeStruct((N_GATE_COLS, rows), F32),
        jax.ShapeDtypeStruct((rows, d_n), BF16),
        jax.ShapeDtypeStruct((rows, A_HEADS * LANES), BF16),
        jax.ShapeDtypeStruct((rows, d_n), BF16),
        jax.ShapeDtypeStruct((rows, A_HEADS * A_DV), BF16),
        jax.ShapeDtypeStruct((rows, LANES), BF16),
    ]
    out_specs = [row_blk(d_qk), row_blk(d_qk), row_blk(d_v), row_blk(d_v), row_blk(2 * d),
                 row_blk(N_GATE_COLS), pl.BlockSpec((N_GATE_COLS, tm), lambda i: (0, i)),
                 row_blk(d_n), row_blk(A_HEADS * LANES), row_blk(d_n), row_blk(A_HEADS * A_DV),
                 row_blk(LANES)]
    in_specs = [row_blk(d), tab_blk, tab_blk] + [_resident(w.shape) for w in weights]
    return pl.pallas_call(
        _in_proj_kernel,
        grid=(rows // tm,),
        in_specs=in_specs,
        out_specs=out_specs,
        out_shape=out_shape,
        compiler_params=pltpu.CompilerParams(dimension_semantics=("parallel",),
                                             vmem_limit_bytes=VMEM_LIMIT),
        name="in_proj",
    )(x2d, cos4, sin4, *weights)


def _mlstm_kernel(qf_ref, kf_ref, vf_ref, gf_ref, gtf_ref,
                  qb_ref, kb_ref, vb_ref, gb_ref, gtb_ref,
                  km_ref, vm_ref, gm_ref, gtm_ref,
                  hf_ref, hb_ref, c_scr, m_scr):
    c_idx = pl.program_id(1)
    lc = qf_ref.shape[0]
    lm = km_ref.shape[0]
    n_pairs = M_HEADS // 2
    half = LANES // 2

    def tri(n, lower):
        r = lax.broadcasted_iota(jnp.int32, (n, n), 0)
        c = lax.broadcasted_iota(jnp.int32, (n, n), 1)
        return (r >= c) if lower else (r <= c)

    lane = lax.broadcasted_iota(jnp.int32, (1, LANES), 1)
    row128 = lax.broadcasted_iota(jnp.int32, (LANES, 1), 0)

    def head_mask(h):
        return (lane < half) if h % 2 == 0 else (lane >= half)

    def state_update(direction, pair, k2, v_pair, cum_col, gl, tot_row):
        ones = jnp.ones(v_pair[0].shape, BF16)
        d_c = None
        decays = []
        for sub in range(2):
            h = 2 * pair + sub
            j = direction * M_HEADS + h
            jf = 2 * M_HEADS + j
            m_prev = m_scr[j][0:1, 0:1]
            tot = tot_row[:, jf:jf + 1]
            g_col = tot - cum_col[:, jf:jf + 1] + gl[:, j:j + 1]
            m_new = jnp.maximum(tot + m_prev, jnp.max(g_col, axis=0, keepdims=True))
            decays.append(jnp.exp(tot + m_prev - m_new))
            wk = jnp.exp(g_col - m_new)
            km = jnp.where(head_mask(h), k2, jnp.zeros_like(k2))
            kw = (km.astype(F32) * wk).astype(BF16)
            vext = jnp.concatenate([v_pair[sub], ones], axis=1)
            upd = _mm_tn(kw, vext)
            d_c = upd if d_c is None else d_c + upd
            m_scr[j] = jnp.broadcast_to(m_new, m_scr.shape[1:])
        decay_rows = jnp.where(row128 < half, decays[0], decays[1])
        ci = direction * n_pairs + pair
        c_scr[ci] = decay_rows * c_scr[ci] + d_c

    @pl.when(c_idx == 0)
    def _init():
        c_scr[...] = jnp.zeros_like(c_scr)
        m_scr[...] = jnp.zeros_like(m_scr)
        glm = gm_ref[...]
        cum_m = _mm_exact(tri(lm, True).astype(F32), glm)
        tot_m = cum_m[lm - 1:lm, :]
        for pair in range(n_pairs):
            k2 = km_ref[:, pair * LANES:(pair + 1) * LANES]
            v_pair = [vm_ref[:, (2 * pair + s) * M_DV:(2 * pair + s + 1) * M_DV] for s in range(2)]
            state_update(0, pair, k2, v_pair, cum_m, glm, tot_m)

    for direction, (q_ref, k_ref, v_ref, g_ref, gt_ref, h_ref) in enumerate(
            ((qf_ref, kf_ref, vf_ref, gf_ref, gtf_ref, hf_ref),
             (qb_ref, kb_ref, vb_ref, gb_ref, gtb_ref, hb_ref))):
        fwd = direction == 0
        gl = g_ref[...]
        glt = gt_ref[...]
        cum_col = _mm_exact(tri(lc, fwd).astype(F32), gl)
        cum_row = _mm_exact(glt, tri(lc, not fwd).astype(F32))
        tot_row = cum_col[lc - 1:lc, :] if fwd else cum_col[0:1, :]
        causal = tri(lc, fwd)
        for pair in range(n_pairs):
            q2 = q_ref[:, pair * LANES:(pair + 1) * LANES]
            k2 = k_ref[:, pair * LANES:(pair + 1) * LANES]
            ci = direction * n_pairs + pair
            c_b = c_scr[ci].astype(BF16)
            v_pair = []
            for sub in range(2):
                h = 2 * pair + sub
                j = direction * M_HEADS + h
                jf = 2 * M_HEADS + j
                v = v_ref[:, h * M_DV:(h + 1) * M_DV]
                v_pair.append(v)
                vext = jnp.concatenate([v, jnp.ones(v.shape, BF16)], axis=1)
                qm = jnp.where(head_mask(h), q2, jnp.zeros_like(q2))
                s = _mm_nt(qm, k2)
                b_col = cum_col[:, jf:jf + 1]
                dmat = b_col - cum_row[jf:jf + 1, :] + glt[j:j + 1, :]
                dmat = jnp.where(causal, dmat, NEG)
                m_prev = m_scr[j][0:1, 0:1]
                m_inter = b_col + m_prev
                m_t = jnp.maximum(m_inter, jnp.max(dmat, axis=-1, keepdims=True))
                w = (jnp.exp(dmat - m_t) * s).astype(BF16)
                acc = jnp.exp(m_inter - m_t) * _mm(qm, c_b) + _mm(w, vext)
                den = jnp.maximum(jnp.abs(acc[:, M_DV:]), jnp.exp(-m_t))
                h_ref[:, h * M_DV:(h + 1) * M_DV] = (acc[:, :M_DV] / den).astype(h_ref.dtype)
            state_update(direction, pair, k2, v_pair, cum_col, gl, tot_row)


def _mlstm(mq, mk, mv, gl, glt, meta, batch, lc):
    rows = mq.shape[0]
    seq = rows // batch
    nc = seq // lc
    km, vm, gm, gtm = meta
    d_qk, d_v = M_HEADS * M_DQK, M_HEADS * M_DV

    fwd_row = lambda b, c: (b * nc + c, 0)
    bwd_row = lambda b, c: (b * nc + nc - 1 - c, 0)
    fwd_col = lambda b, c: (0, b * nc + c)
    bwd_col = lambda b, c: (0, b * nc + nc - 1 - c)

    def views(row_map, col_map):
        return [pl.BlockSpec((lc, d_qk), row_map), pl.BlockSpec((lc, d_qk), row_map),
                pl.BlockSpec((lc, d_v), row_map), pl.BlockSpec((lc, N_GATE_COLS), row_map),
                pl.BlockSpec((N_GATE_COLS, lc), col_map)]

    const = lambda a: pl.BlockSpec(a.shape, lambda b, c: (0, 0))
    in_specs = views(fwd_row, fwd_col) + views(bwd_row, bwd_col) + [const(a) for a in meta]
    return pl.pallas_call(
        _mlstm_kernel,
        grid=(batch, nc),
        in_specs=in_specs,
        out_specs=[pl.BlockSpec((lc, d_v), fwd_row), pl.BlockSpec((lc, d_v), bwd_row)],
        out_shape=[jax.ShapeDtypeStruct((rows, d_v), BF16)] * 2,
        scratch_shapes=[pltpu.VMEM((2 * (M_HEADS // 2), LANES, 2 * M_DV), F32),
                        pltpu.VMEM((2 * M_HEADS, 8, LANES), F32)],
        compiler_params=pltpu.CompilerParams(dimension_semantics=("parallel", "arbitrary"),
                                             vmem_limit_bytes=VMEM_LIMIT),
        name="mlstm",
    )(mq, mk, mv, gl, glt, mq, mk, mv, gl, glt, km, vm, gm, gtm)


def _attn_kernel(qn_ref, qr_ref, kn_ref, kr_ref, v_ref, knm_ref, krm_ref, vm_ref, bias_ref,
                 o_ref, *, tk):
    q = jnp.concatenate([qn_ref[...], qr_ref[...]], axis=1)

    def vext(v):
        return jnp.concatenate([v, jnp.ones(v.shape, BF16)], axis=1)

    km = jnp.concatenate([knm_ref[...], krm_ref[...]], axis=1)
    s = _mm_nt(q, km) + bias_ref[...]
    m = jnp.max(s, axis=-1, keepdims=True)
    acc = _mm(jnp.exp(s - m).astype(BF16), vext(vm_ref[...]))
    for j in range(kn_ref.shape[0] // tk):
        rows = slice(j * tk, (j + 1) * tk)
        k = jnp.concatenate([kn_ref[rows, :], kr_ref[rows, :]], axis=1)
        s = _mm_nt(q, k)
        m_new = jnp.maximum(m, jnp.max(s, axis=-1, keepdims=True))
        p = jnp.exp(s - m_new).astype(BF16)
        acc = jnp.exp(m - m_new) * acc + _mm(p, vext(v_ref[rows, :]))
        m = m_new
    o_ref[...] = (acc[:, :A_DV] / acc[:, A_DV:]).astype(o_ref.dtype)


def _attention(qn, qr, kn, kr, va, meta, batch, tq, tk):
    rows = qn.shape[0]
    seq = rows // batch
    nq = seq // tq
    knm, krm, vam, bias = meta
    n_meta_pad = knm.shape[0]
    q_blk = pl.BlockSpec((tq, LANES), lambda b, h, qi: (b * nq + qi, h))
    k_blk = pl.BlockSpec((seq, LANES), lambda b, h, qi: (b, h))
    kr_blk = pl.BlockSpec((seq, LANES), lambda b, h, qi: (b, 0))
    meta_h = pl.BlockSpec((n_meta_pad, LANES), lambda b, h, qi: (0, h))
    meta_0 = pl.BlockSpec((n_meta_pad, LANES), lambda b, h, qi: (0, 0))
    bias_blk = pl.BlockSpec((1, n_meta_pad), lambda b, h, qi: (0, 0))
    return pl.pallas_call(
        functools.partial(_attn_kernel, tk=tk),
        grid=(batch, A_HEADS, nq),
        in_specs=[q_blk, q_blk, k_blk, kr_blk, k_blk, meta_h, meta_0, meta_h, bias_blk],
        out_specs=q_blk,
        out_shape=jax.ShapeDtypeStruct((rows, A_HEADS * A_DV), BF16),
        compiler_params=pltpu.CompilerParams(
            dimension_semantics=("parallel", "parallel", "parallel"),
            vmem_limit_bytes=VMEM_LIMIT),
        name="attention",
    )(qn, qr, kn, kr, va, knm, krm, vam, bias)


def _mix_kernel(x_ref, hf_ref, hb_ref, so_ref, att_ref, sg_ref, gm_ref, wm_ref, wa_ref, wo_ref,
                gffn_ref, wrh_ref, wrl_ref, br_ref, h1_ref, u2_ref, idx_ref, gate_ref, *, n_split):
    d = x_ref.shape[1]
    tr = x_ref.shape[0] // n_split
    for part in range(n_split):
        rows = slice(part * tr, (part + 1) * tr)
        hs = hf_ref[rows, :].astype(F32) + hb_ref[rows, :].astype(F32)
        parts = []
        for h in range(M_HEADS):
            blk = hs[:, h * M_DV:(h + 1) * M_DV]
            parts.append(blk * lax.rsqrt(jnp.mean(blk * blk, axis=-1, keepdims=True) + EPS))
        hn = jnp.concatenate(parts, axis=1) * gm_ref[...] * so_ref[rows, :].astype(F32)
        y_m = _mm(hn.astype(BF16), wm_ref[...])
        y_a = _mm(att_ref[rows, :], wa_ref[...])
        sg = sg_ref[rows, :].astype(F32)
        mixed = sg[:, :d] * y_m + sg[:, d:] * y_a
        h1 = x_ref[rows, :] + _mm(mixed.astype(BF16), wo_ref[...])
        h1_ref[rows, :] = h1
        u2 = _rms(h1, gffn_ref[...])
        u2_hi = u2.astype(BF16)
        u2_ref[rows, :] = u2_hi
        u2_lo = (u2 - u2_hi.astype(F32)).astype(BF16)
        logits = (_mm(u2_hi, wrh_ref[...]) + _mm(u2_hi, wrl_ref[...]) + _mm(u2_lo, wrh_ref[...])
                  + br_ref[...])
        lane = lax.broadcasted_iota(jnp.int32, logits.shape, 1).astype(F32)
        out_lane = lax.broadcasted_iota(jnp.int32, (tr, TOP_K), 1)
        work = logits
        top_idx = jnp.zeros((tr, TOP_K), F32)
        top_exp = jnp.zeros((tr, TOP_K), F32)
        v_max = None
        for k in range(TOP_K):
            v = jnp.max(work, axis=-1, keepdims=True)
            idx = jnp.min(jnp.where(work == v, lane, float(N_EXPERTS)), axis=-1, keepdims=True)
            work = jnp.where(lane == idx, -jnp.inf, work)
            v_max = v if k == 0 else v_max
            top_idx = jnp.where(out_lane == k, idx, top_idx)
            top_exp = jnp.where(out_lane == k, jnp.exp(v - v_max), top_exp)
        idx_ref[rows, :] = top_idx.astype(jnp.int32)
        gate_ref[rows, :] = top_exp / jnp.sum(top_exp, axis=-1, keepdims=True)


def _mix(x2d, hf, hb, so, att, sg, weights, tm):
    rows, d = x2d.shape
    row_blk = lambda w: pl.BlockSpec((tm, w), lambda i: (i, 0))
    in_specs = ([row_blk(d), row_blk(d), row_blk(d), row_blk(d), row_blk(d), row_blk(2 * d)]
                + [_resident(w.shape) for w in weights])
    return pl.pallas_call(
        functools.partial(_mix_kernel, n_split=2 if tm % 32 == 0 else 1),
        grid=(rows // tm,),
        in_specs=in_specs,
        out_specs=[row_blk(d), row_blk(d), row_blk(TOP_K), row_blk(TOP_K)],
        out_shape=[jax.ShapeDtypeStruct((rows, d), F32), jax.ShapeDtypeStruct((rows, d), BF16),
                   jax.ShapeDtypeStruct((rows, TOP_K), jnp.int32),
                   jax.ShapeDtypeStruct((rows, TOP_K), F32)],
        compiler_params=pltpu.CompilerParams(dimension_semantics=("parallel",),
                                             vmem_limit_bytes=VMEM_LIMIT),
        name="mix",
    )(x2d, hf, hb, so, att, sg, *weights)


def _expert_kernel(be_ref, nused_ref, x_ref, w1_ref, b1_ref, w2_ref, b2_ref, y_ref, w1b_scr, w2b_scr):
    i = pl.program_id(0)

    @pl.when(i < nused_ref[0])
    def _():
        @pl.when((i == 0) | (be_ref[i] != be_ref[jnp.maximum(i - 1, 0)]))
        def _cast():
            w1b_scr[...] = w1_ref[...].astype(BF16)
            w2b_scr[...] = w2_ref[...].astype(BF16)

        hid = _mm(x_ref[...], w1b_scr[...]) + b1_ref[...]
        glu = jnp.minimum(hid[:, :D_FF], SWIGLU_LIMIT)
        lin = jnp.clip(hid[:, D_FF:], -SWIGLU_LIMIT, SWIGLU_LIMIT)
        act = glu * _sigmoid(SWIGLU_ALPHA * glu) * (lin + 1.0)
        y_ref[...] = (_mm(act.astype(BF16), w2b_scr[...]) + b2_ref[...]).astype(y_ref.dtype)


def _experts(block_e, n_used, xg, w1, b1, w2, b2, tme):
    rows, d = xg.shape
    nb = rows // tme
    grid_spec = pltpu.PrefetchScalarGridSpec(
        num_scalar_prefetch=2,
        grid=(nb,),
        in_specs=[
            pl.BlockSpec((tme, d), lambda i, be, nu: (i, 0)),
            pl.BlockSpec((None, d, 2 * D_FF), lambda i, be, nu: (be[i], 0, 0)),
            pl.BlockSpec((None, 1, 2 * D_FF), lambda i, be, nu: (be[i], 0, 0)),
            pl.BlockSpec((None, D_FF, d), lambda i, be, nu: (be[i], 0, 0)),
            pl.BlockSpec((None, 1, d), lambda i, be, nu: (be[i], 0, 0)),
        ],
        out_specs=pl.BlockSpec((tme, d), lambda i, be, nu: (i, 0)),
        scratch_shapes=[pltpu.VMEM((d, 2 * D_FF), BF16), pltpu.VMEM((D_FF, d), BF16)],
    )
    return pl.pallas_call(
        _expert_kernel,
        grid_spec=grid_spec,
        out_shape=jax.ShapeDtypeStruct((rows, d), BF16),
        compiler_params=pltpu.CompilerParams(dimension_semantics=("arbitrary",),
                                             vmem_limit_bytes=VMEM_LIMIT),
        name="experts",
    )(block_e, n_used, xg, w1, b1, w2, b2)


def _final_kernel(h1_ref, yg_ref, gate_ref, g_ref, o_ref):
    gates = gate_ref[...]
    y = h1_ref[...]
    for k in range(TOP_K):
        y = y + yg_ref[k].astype(F32) * gates[:, k:k + 1]
    o_ref[...] = _rms(y, g_ref[...])


def _final(h1, yg, gates, g, tm):
    rows, d = h1.shape
    row_blk = pl.BlockSpec((tm, d), lambda i: (i, 0))
    return pl.pallas_call(
        _final_kernel,
        grid=(rows // tm,),
        in_specs=[row_blk, pl.BlockSpec((TOP_K, tm, d), lambda i: (0, i, 0)),
                  pl.BlockSpec((tm, TOP_K), lambda i: (i, 0)), pl.BlockSpec((1, d), lambda i: (0, 0))],
        out_specs=row_blk,
        out_shape=jax.ShapeDtypeStruct((rows, d), F32),
        compiler_params=pltpu.CompilerParams(dimension_semantics=("parallel",)),
        name="final",
    )(h1, yg, gates, g)


def _rope_tables(pos):
    half = A_ROPE // 2
    freqs = ROPE_THETA ** (-jnp.arange(half, dtype=F32) / half)
    ang = pos.astype(F32)[:, None] * freqs[None, :]
    zeros = jnp.zeros((pos.shape[0], LANES - A_ROPE), F32)
    cos, sin = jnp.cos(ang), jnp.sin(ang)
    return jnp.concatenate([cos, cos, zeros], axis=1), jnp.concatenate([sin, sin, zeros], axis=1)


def _rope_pair(w):
    half = A_ROPE // 2
    zeros = jnp.zeros(w.shape[:-1] + (LANES - A_ROPE,), w.dtype)
    a = jnp.concatenate([w, zeros], axis=-1)
    b = jnp.concatenate([-w[..., half:], w[..., :half], zeros], axis=-1)
    return a, b


def _pick_tile(n, prefs):
    for t in prefs:
        if n % t == 0:
            return t
    return n


def kernel(x, meta_tokens, norm_mix, w_in, m_b_i, m_b_f, m_norm, w_m_proj, a_w_uq, a_w_ukv,
           a_norm_q, a_norm_kv, w_a_proj, w_out, norm_ffn, w_router, b_router, w1, b1, w2, b2,
           norm_final):
    assert norm_mix.shape[0] == 1, "single-layer encoder"
    batch, seq, d = x.shape
    rows = batch * seq
    x2d = x.reshape(rows, d)

    sizes = (M_HEADS * M_DQK, M_HEADS * M_DQK, M_HEADS * M_DV, M_HEADS * M_DV, 2 * M_HEADS,
             2 * M_HEADS, Q_LORA, KV_LORA, A_ROPE, 2 * d)
    offs = np.cumsum((0,) + sizes)
    w_q, w_k, w_v, w_o, w_i, w_f, w_cq, w_ckv, w_kr, w_g = (
        w_in[0][:, offs[i]:offs[i + 1]] for i in range(len(sizes)))
    kr_a, kr_b = _rope_pair(w_kr)
    w_qk = jnp.concatenate([w_q * (M_DQK ** -0.5), w_k], axis=1).astype(BF16)
    w_vo = jnp.concatenate([w_v, w_o], axis=1).astype(BF16)
    w_lat = jnp.concatenate([w_cq, w_ckv, kr_a, kr_b], axis=1).astype(BF16)
    w_if = jnp.concatenate([w_i, w_f], axis=1).astype(BF16)
    b_if = jnp.concatenate([m_b_i[0].reshape(-1), m_b_f[0].reshape(-1)]).astype(F32)
    uq_a, uq_b = _rope_pair(a_w_uq[0][:, :, A_NOPE:])
    w_uq = jnp.concatenate([a_w_uq[0][:, :, :A_NOPE].reshape(Q_LORA, -1),
                            uq_a.reshape(Q_LORA, -1), uq_b.reshape(Q_LORA, -1)], axis=1).astype(BF16)
    w_ukv = jnp.concatenate([a_w_ukv[0][:, :, :A_NOPE].reshape(KV_LORA, -1),
                             a_w_ukv[0][:, :, A_NOPE:].reshape(KV_LORA, -1)], axis=1).astype(BF16)
    in_weights = (norm_mix[0].reshape(1, d), w_qk, w_vo, w_g.astype(BF16), w_lat,
                  w_if, w_if.T, b_if.reshape(1, -1), b_if.reshape(-1, 1),
                  a_norm_q[0].reshape(1, -1), a_norm_kv[0].reshape(1, -1), w_uq, w_ukv)

    cos_r, sin_r = _rope_tables(jnp.arange(N_META, N_META + seq))
    cos_m, sin_m = _rope_tables(jnp.arange(N_META))
    tm = _pick_tile(seq, (256, 128, 64, 32, 16))
    (mq, mk, mv, so, sg, gl, glt, qn, qr, kn, va, kr) = _in_proj(x2d, cos_r, sin_r, in_weights, tm)
    (_, mk_m, mv_m, _, _, gl_m, glt_m, _, _, kn_m, va_m, kr_m) = _in_proj(
        meta_tokens.astype(x.dtype), cos_m, sin_m, in_weights, N_META)

    n_pad = LANES - N_META
    pad_gate = jnp.concatenate([jnp.full((n_pad, 2 * M_HEADS), NEG, F32),
                                jnp.zeros((n_pad, 2 * M_HEADS), F32)], axis=1)
    front = lambda a: jnp.concatenate([jnp.zeros((n_pad, a.shape[1]), a.dtype), a], axis=0)
    mlstm_meta = (front(mk_m), front(mv_m), jnp.concatenate([pad_gate, gl_m], axis=0),
                  jnp.concatenate([pad_gate.T, glt_m], axis=1))
    lc = _pick_tile(seq, (256, 128))
    hf, hb = _mlstm(mq, mk, mv, gl, glt, mlstm_meta, batch, lc)

    back = lambda a: jnp.concatenate([a, jnp.zeros((n_pad, a.shape[1]), a.dtype)], axis=0)
    bias = jnp.concatenate([jnp.zeros((1, N_META), F32), jnp.full((1, n_pad), NEG, F32)], axis=1)
    tq = _pick_tile(seq, (512, 256, 128))
    tk = _pick_tile(seq, (512, 256, 128))
    att = _attention(qn, qr, kn, kr, va, (back(kn_m), back(kr_m), back(va_m), bias), batch, tq, tk)

    wr_hi = w_router[0].astype(BF16)
    wr_lo = (w_router[0] - wr_hi.astype(F32)).astype(BF16)
    mix_weights = (m_norm[0].reshape(1, -1), w_m_proj[0].astype(BF16), w_a_proj[0].astype(BF16),
                   w_out[0].astype(BF16), norm_ffn[0].reshape(1, d), wr_hi, wr_lo,
                   b_router[0].reshape(1, -1))
    h1, u2, top_idx, gates = _mix(x2d, hf, hb, so, att, sg, mix_weights,
                                  _pick_tile(rows, (512, 256, 128, 64, 32, 16)))

    tme = 512
    n = rows * TOP_K
    nb = -(-n // tme) + N_EXPERTS
    i32 = jnp.int32
    flat_e = top_idx.reshape(-1)
    pair_ids = jnp.arange(n, dtype=i32)
    _, order = lax.sort((flat_e, pair_ids), num_keys=1, is_stable=True)
    _, rank = lax.sort((order, pair_ids), num_keys=1)
    counts = jnp.sum(flat_e[:, None] == jnp.arange(N_EXPERTS, dtype=i32)[None, :], axis=0, dtype=i32)
    start = jnp.cumsum(counts) - counts
    padded = (counts + tme - 1) // tme * tme
    padded_end = jnp.cumsum(padded)
    padded_start = padded_end - padded
    block_row0 = jnp.arange(nb, dtype=i32) * tme
    block_e = jnp.minimum(jnp.sum(padded_end[None, :] <= block_row0[:, None], axis=1, dtype=i32),
                          N_EXPERTS - 1)
    n_used = (padded_end[-1] // tme).astype(i32).reshape(1)
    off = (block_row0 - padded_start[block_e])[:, None] + jnp.arange(tme, dtype=i32)[None, :]
    valid = off < counts[block_e][:, None]
    src = jnp.clip(start[block_e][:, None] + off, 0, n - 1)
    filler = (block_row0[:, None] + jnp.arange(tme, dtype=i32)[None, :]) % rows
    buf_tok = jnp.where(valid, order[src] // TOP_K, filler).reshape(-1)
    slot = padded_start[flat_e] + rank - start[flat_e]
    xg = u2[buf_tok]
    yb = _experts(block_e, n_used, xg, w1[0], b1[0].reshape(N_EXPERTS, 1, -1), w2[0],
                  b2[0].reshape(N_EXPERTS, 1, -1), tme)
    yg = yb[slot.reshape(rows, TOP_K).T.reshape(-1)].reshape(TOP_K, rows, d)

    out = _final(h1, yg, gates, norm_final.reshape(1, d), tm)
    return out.reshape(batch, seq, d)
```

```python
import functools

import jax
import jax.numpy as jnp
import numpy as np
from jax import lax
from jax.experimental import pallas as pl
from jax.experimental.pallas import tpu as pltpu

N_META = 16
EPS = 1e-6
NEG = -1e30
M_HEADS = 8
M_DQK = 64
M_DV = 128
A_HEADS = 8
A_NOPE = 128
A_ROPE = 64
A_DV = 128
Q_LORA = 256
KV_LORA = 128
ROPE_THETA = 10000.0
N_EXPERTS = 32
TOP_K = 4
D_FF = 1024
SWIGLU_ALPHA = 1.702
SWIGLU_LIMIT = 7.0

LANES = 128
ONES_ROWS = 16
N_GATE_COLS = 4 * M_HEADS
ATT_SCALE = (A_NOPE + A_ROPE) ** -0.5 * 1.4426950408889634
VMEM_LIMIT = 56 * 1024 * 1024

F32 = jnp.float32
BF16 = jnp.bfloat16


def _mm(a, b):
    return jnp.dot(a, b, preferred_element_type=F32)


def _mm_nt(a, b):
    return lax.dot_general(a, b, (((1,), (1,)), ((), ())), preferred_element_type=F32)


def _mm_tn(a, b):
    return lax.dot_general(a, b, (((0,), (0,)), ((), ())), preferred_element_type=F32)


def _mm_exact(a, b):
    return jnp.dot(a, b, preferred_element_type=F32, precision=lax.Precision.HIGHEST)


def _rms(x, g):
    return x * lax.rsqrt(jnp.mean(x * x, axis=-1, keepdims=True) + EPS) * g


def _log_sigmoid(x):
    return jnp.minimum(x, 0.0) - jnp.log1p(jnp.exp(-jnp.abs(x)))


def _sigmoid(x):
    return 1.0 / (1.0 + jnp.exp(-x))


def _resident(shape):
    nd = len(shape)
    return pl.BlockSpec(shape, lambda *_: (0,) * nd, pipeline_mode=pl.Buffered(1))


def _in_proj_kernel(x_ref, cos_ref, sin_ref, gmix_ref, wqvt_ref, wko_ref, wgate_ref, wlat_ref,
                    wif_ref, wift_ref, bif_ref, bift_ref, gq_ref, gkv_ref, wuq_ref, wukv_ref,
                    mqt_ref, mk_ref, mvt_ref, so_ref, sg_ref, gl_ref, glt_ref,
                    qn_ref, qr_ref, kn_ref, va_ref, kr_ref):
    d_qk = M_HEADS * M_DQK
    ub = _rms(x_ref[...], gmix_ref[...]).astype(BF16)

    qvt = _mm_nt(wqvt_ref[...], ub)
    mqt_ref[...] = qvt[:d_qk, :].astype(BF16)
    mvt_ref[...] = qvt[d_qk:, :].astype(BF16)
    ko = _mm(ub, wko_ref[...])
    mk_ref[...] = ko[:, :d_qk].astype(BF16)
    so_ref[...] = _sigmoid(ko[:, d_qk:]).astype(BF16)
    sg_ref[...] = _sigmoid(_mm(ub, wgate_ref[...])).astype(BF16)

    gi = _mm(ub, wif_ref[...]) + bif_ref[...]
    col = lax.broadcasted_iota(jnp.int32, gi.shape, 1)
    gl_ref[...] = jnp.where(col < 2 * M_HEADS, gi, _log_sigmoid(gi))
    git = _mm_nt(wift_ref[...], ub) + bift_ref[...]
    row = lax.broadcasted_iota(jnp.int32, git.shape, 0)
    glt_ref[...] = jnp.where(row < 2 * M_HEADS, git, _log_sigmoid(git))

    lat = _mm(ub, wlat_ref[...])
    cos = cos_ref[...]
    sin = sin_ref[...]
    kr_ref[...] = (lat[:, Q_LORA + KV_LORA:Q_LORA + KV_LORA + LANES] * cos
                   + lat[:, Q_LORA + KV_LORA + LANES:] * sin).astype(BF16)

    cqn = _rms(lat[:, :Q_LORA], gq_ref[...]).astype(BF16)
    q3 = _mm(cqn, wuq_ref[...])
    d_n = A_HEADS * A_NOPE
    qn_ref[...] = (q3[:, :d_n] * ATT_SCALE).astype(BF16)
    for h in range(A_HEADS):
        a = q3[:, d_n + h * LANES:d_n + (h + 1) * LANES]
        b = q3[:, d_n + (A_HEADS + h) * LANES:d_n + (A_HEADS + h + 1) * LANES]
        qr_ref[:, h * LANES:(h + 1) * LANES] = ((a * cos + b * sin) * ATT_SCALE).astype(BF16)

    ckvn = _rms(lat[:, Q_LORA:Q_LORA + KV_LORA], gkv_ref[...]).astype(BF16)
    kv = _mm(ckvn, wukv_ref[...])
    kn_ref[...] = kv[:, :d_n].astype(BF16)
    va_ref[...] = kv[:, d_n:].astype(BF16)


def _in_proj(x2d, cos4, sin4, weights, tm):
    rows, d = x2d.shape
    n_tab = cos4.shape[0] // tm
    row_blk = lambda w: pl.BlockSpec((tm, w), lambda i: (i, 0))
    tab_blk = pl.BlockSpec((tm, LANES), lambda i: (i % n_tab, 0))
    d_qk, d_v, d_n = M_HEADS * M_DQK, M_HEADS * M_DV, A_HEADS * A_NOPE
    out_shape = [
        jax.ShapeDtypeStruct((d_qk, rows), BF16),
        jax.ShapeDtypeStruct((rows, d_qk), BF16),
        jax.ShapeDtypeStruct((d_v, rows), BF16),
        jax.ShapeDtypeStruct((rows, d_v), BF16),
        jax.ShapeDtypeStruct((rows, 2 * d), BF16),
        jax.ShapeDtypeStruct((rows, N_GATE_COLS), F32),
        jax.ShapeDtypeStruct((N_GATE_COLS, rows), F32),
        jax.ShapeDtypeStruct((rows, d_n), BF16),
        jax.ShapeDtypeStruct((rows, A_HEADS * LANES), BF16),
        jax.ShapeDtypeStruct((rows, d_n), BF16),
        jax.ShapeDtypeStruct((rows, A_HEADS * A_DV), BF16),
        jax.ShapeDtypeStruct((rows, LANES), BF16),
    ]
    col_blk = lambda h: pl.BlockSpec((h, tm), lambda i: (0, i))
    out_specs = [col_blk(d_qk), row_blk(d_qk), col_blk(d_v), row_blk(d_v), row_blk(2 * d),
                 row_blk(N_GATE_COLS), pl.BlockSpec((N_GATE_COLS, tm), lambda i: (0, i)),
                 row_blk(d_n), row_blk(A_HEADS * LANES), row_blk(d_n), row_blk(A_HEADS * A_DV),
                 row_blk(LANES)]
    in_specs = [row_blk(d), tab_blk, tab_blk] + [_resident(w.shape) for w in weights]
    return pl.pallas_call(
        _in_proj_kernel,
        grid=(rows // tm,),
        in_specs=in_specs,
        out_specs=out_specs,
        out_shape=out_shape,
        compiler_params=pltpu.CompilerParams(dimension_semantics=("parallel",),
                                             vmem_limit_bytes=VMEM_LIMIT),
        name="in_proj",
    )(x2d, cos4, sin4, *weights)


def _mlstm_kernel(qf_ref, kf_ref, vf_ref, gf_ref, gtf_ref,
                  qb_ref, kb_ref, vb_ref, gb_ref, gtb_ref,
                  km_ref, vm_ref, gtm_ref,
                  hf_ref, hb_ref, c_scr, m_scr):
    c_idx = pl.program_id(1)
    lc = kf_ref.shape[0]
    lm = km_ref.shape[0]
    n_pairs = M_HEADS // 2
    half = LANES // 2
    log2e = 1.4426950408889634

    def tri(n, lower):
        r = lax.broadcasted_iota(jnp.int32, (n, n), 0)
        c = lax.broadcasted_iota(jnp.int32, (n, n), 1)
        return (r >= c) if lower else (r <= c)

    def as_bf16(mask):
        return jnp.where(mask, 1.0, 0.0).astype(BF16)

    def eye(n):
        r = lax.broadcasted_iota(jnp.int32, (n, n), 0)
        c = lax.broadcasted_iota(jnp.int32, (n, n), 1)
        return as_bf16(r == c)

    eye_t = eye(lc)
    lane = lax.broadcasted_iota(jnp.int32, (1, LANES), 1)
    sub128 = lax.broadcasted_iota(jnp.int32, (LANES, 1), 0)

    def lane_mask(h):
        return (lane < half) if h % 2 == 0 else (lane >= half)

    def sublane_mask(h):
        return (sub128 < half) if h % 2 == 0 else (sub128 >= half)

    def split3(x):
        hi = x.astype(BF16)
        r1 = x - hi.astype(F32)
        mid = r1.astype(BF16)
        lo = (r1 - mid.astype(F32)).astype(BF16)
        return hi, mid, lo

    def cum_rows(g_rows, tri_b):
        hi, mid, lo = split3(g_rows)
        return _mm(hi, tri_b) + _mm(mid, tri_b) + _mm(lo, tri_b)

    def cum_cols(tri_b, g_cols):
        hi, mid, lo = split3(g_cols)
        return _mm(tri_b, hi) + _mm(tri_b, mid) + _mm(tri_b, lo)

    def state_update(direction, pair, k2, vt_pair, cum_row, g_rows, tot_col):
        n = k2.shape[0]
        d_c = None
        decays = []
        for sub in range(2):
            h = 2 * pair + sub
            j = direction * M_HEADS + h
            jf = 2 * M_HEADS + j
            m_prev = m_scr[j][0:1, 0:1]
            tot = tot_col[jf:jf + 1, :]
            g_row = tot - cum_row[jf:jf + 1, :] + g_rows[j:j + 1, :]
            m_new = jnp.maximum(tot + m_prev, jnp.max(g_row, axis=1, keepdims=True))
            decays.append(jnp.exp2(tot + m_prev - m_new))
            wk = jnp.exp2(g_row - m_new)
            vw = jnp.concatenate([(vt_pair[sub].astype(F32) * wk).astype(BF16),
                                  jnp.broadcast_to(wk.astype(BF16), (ONES_ROWS, n))], axis=0)
            km = jnp.where(lane_mask(h), k2, jnp.zeros_like(k2))
            upd = _mm(vw, km)
            d_c = upd if d_c is None else d_c + upd
            m_scr[j] = jnp.broadcast_to(m_new, m_scr.shape[1:])
        decay_cols = jnp.where(lane < half, decays[0], decays[1])
        ci = direction * n_pairs + pair
        c_scr[ci] = decay_cols * c_scr[ci] + d_c

    @pl.when(c_idx == 0)
    def _init():
        c_scr[...] = jnp.zeros_like(c_scr)
        m_scr[...] = jnp.zeros_like(m_scr)
        g_rows = gtm_ref[...] * log2e
        cum_m = cum_rows(g_rows, as_bf16(tri(lm, False)))
        tot_m = cum_m[:, lm - 1:lm]
        for pair in range(n_pairs):
            k2 = km_ref[:, pair * LANES:(pair + 1) * LANES]
            vt_pair = [vm_ref[(2 * pair + s) * M_DV:(2 * pair + s + 1) * M_DV, :] for s in range(2)]
            state_update(0, pair, k2, vt_pair, cum_m, g_rows, tot_m)

    views = ((qf_ref, kf_ref, vf_ref, gf_ref, gtf_ref, hf_ref),
             (qb_ref, kb_ref, vb_ref, gb_ref, gtb_ref, hb_ref))
    shared = []
    for direction, (_, _, _, g_ref, gt_ref, _) in enumerate(views):
        fwd = direction == 0
        g_cols = g_ref[...] * log2e
        g_rows = gt_ref[...] * log2e
        cum_row = cum_rows(g_rows, as_bf16(tri(lc, not fwd)))
        cum_col = cum_cols(as_bf16(tri(lc, fwd)), g_cols)
        tot_col = cum_row[:, lc - 1:lc] if fwd else cum_row[:, 0:1]
        key_cols = g_cols[:, :2 * M_HEADS] - cum_col[:, 2 * M_HEADS:]
        valid = tri(lc, not fwd)
        shared.append((g_rows, cum_row, tot_col, key_cols, valid))

    units = [(d, p, s) for d in range(2) for p in range(n_pairs) for s in range(2)]
    pair_vals = {}
    vals = {}

    def scores(u):
        direction, pair, sub = units[u]
        q_ref, k_ref, v_ref = views[direction][:3]
        h = 2 * pair + sub
        if sub == 0:
            q2t = q_ref[pair * LANES:(pair + 1) * LANES, :]
            k2 = k_ref[:, pair * LANES:(pair + 1) * LANES]
            c_b = c_scr[direction * n_pairs + pair].astype(BF16)
            pair_vals[(direction, pair)] = (q2t, k2, c_b, [])
        q2t, k2, _, vt_pair = pair_vals[(direction, pair)]
        vt = v_ref[h * M_DV:(h + 1) * M_DV, :]
        vt_pair.append(vt)
        qmt = jnp.where(sublane_mask(h), q2t, jnp.zeros_like(q2t))
        vals[u] = dict(vt=vt, qmt=qmt, st=_mm(k2, qmt))

    def accumulate(u):
        direction, pair, sub = units[u]
        g_rows, cum_row, tot_col, key_cols, valid = shared[direction]
        _, k2, c_b, vt_pair = pair_vals[(direction, pair)]
        j = direction * M_HEADS + 2 * pair + sub
        jf = 2 * M_HEADS + j
        v = vals[u]
        b_row = cum_row[jf:jf + 1, :]
        dt = jnp.where(valid, key_cols[:, j:j + 1] + b_row, NEG)
        m_inter = b_row + m_scr[j][0:1, 0:1]
        m_t = jnp.maximum(m_inter, jnp.max(dt, axis=0, keepdims=True))
        wt = (jnp.exp2(dt - m_t) * v['st']).astype(BF16)
        qs = (v['qmt'].astype(F32) * jnp.exp2(m_inter - m_t)).astype(BF16)
        vext = jnp.concatenate([v['vt'], jnp.ones((ONES_ROWS, lc), BF16)], axis=0)
        v['acc'] = _mm(vext, wt) + _mm(c_b, qs)
        v['m_t'] = m_t
        if sub == 1:
            state_update(direction, pair, k2, vt_pair, cum_row, g_rows, tot_col)

    def finish(u):
        direction, pair, sub = units[u]
        h_ref = views[direction][5]
        h = 2 * pair + sub
        v = vals.pop(u)
        den = jnp.maximum(jnp.abs(v['acc'][M_DV:M_DV + 1, :]), jnp.exp2(-v['m_t']))
        ht = (v['acc'][:M_DV, :] * (1.0 / den)).astype(BF16)
        h_ref[:, h * M_DV:(h + 1) * M_DV] = _mm_nt(eye_t, ht).astype(h_ref.dtype)

    for t in range(len(units) + 2):
        if t < len(units):
            scores(t)
        if 0 <= t - 1 < len(units):
            accumulate(t - 1)
        if 0 <= t - 2 < len(units):
            finish(t - 2)


def _mlstm(mq, mk, mv, gl, glt, meta, batch, lc):
    rows = mk.shape[0]
    seq = rows // batch
    nc = seq // lc
    km, vm, gtm = meta
    d_qk, d_v = M_HEADS * M_DQK, M_HEADS * M_DV

    fwd_row = lambda b, c: (b * nc + c, 0)
    bwd_row = lambda b, c: (b * nc + nc - 1 - c, 0)
    fwd_col = lambda b, c: (0, b * nc + c)
    bwd_col = lambda b, c: (0, b * nc + nc - 1 - c)

    def views(row_map, col_map):
        return [pl.BlockSpec((d_qk, lc), col_map), pl.BlockSpec((lc, d_qk), row_map),
                pl.BlockSpec((d_v, lc), col_map), pl.BlockSpec((lc, N_GATE_COLS), row_map),
                pl.BlockSpec((N_GATE_COLS, lc), col_map)]

    const = lambda a: pl.BlockSpec(a.shape, lambda b, c: (0, 0))
    in_specs = views(fwd_row, fwd_col) + views(bwd_row, bwd_col) + [const(a) for a in meta]
    return pl.pallas_call(
        _mlstm_kernel,
        grid=(batch, nc),
        in_specs=in_specs,
        out_specs=[pl.BlockSpec((lc, d_v), fwd_row), pl.BlockSpec((lc, d_v), bwd_row)],
        out_shape=[jax.ShapeDtypeStruct((rows, d_v), BF16)] * 2,
        scratch_shapes=[pltpu.VMEM((2 * (M_HEADS // 2), M_DV + ONES_ROWS, LANES), F32),
                        pltpu.VMEM((2 * M_HEADS, 8, LANES), F32)],
        compiler_params=pltpu.CompilerParams(dimension_semantics=("parallel", "arbitrary"),
                                             vmem_limit_bytes=VMEM_LIMIT),
        name="mlstm",
    )(mq, mk, mv, gl, glt, mq, mk, mv, gl, glt, km, vm, gtm)


def _attn_kernel(qn_ref, qr_ref, kn_ref, kr_ref, v_ref, knm_ref, krm_ref, vm_ref, bias_ref,
                 o_ref, *, tk):
    q = jnp.concatenate([qn_ref[...], qr_ref[...]], axis=1)

    def vext(v):
        return jnp.concatenate([v, jnp.ones(v.shape, BF16)], axis=1)

    km = jnp.concatenate([knm_ref[...], krm_ref[...]], axis=1)
    s = _mm_nt(q, km) + bias_ref[...]
    m = jnp.max(s, axis=-1, keepdims=True)
    acc = _mm(jnp.exp2(s - m).astype(BF16), vext(vm_ref[...]))
    for j in range(kn_ref.shape[0] // tk):
        rows = slice(j * tk, (j + 1) * tk)
        k = jnp.concatenate([kn_ref[rows, :], kr_ref[rows, :]], axis=1)
        s = _mm_nt(q, k)
        m_new = jnp.maximum(m, jnp.max(s, axis=-1, keepdims=True))
        p = jnp.exp2(s - m_new).astype(BF16)
        acc = jnp.exp2(m - m_new) * acc + _mm(p, vext(v_ref[rows, :]))
        m = m_new
    o_ref[...] = (acc[:, :A_DV] / acc[:, A_DV:]).astype(o_ref.dtype)


def _attention(qn, qr, kn, kr, va, meta, batch, tq, tk):
    rows = qn.shape[0]
    seq = rows // batch
    nq = seq // tq
    knm, krm, vam, bias = meta
    n_meta_pad = knm.shape[0]
    q_blk = pl.BlockSpec((tq, LANES), lambda b, h, qi: (b * nq + qi, h))
    k_blk = pl.BlockSpec((seq, LANES), lambda b, h, qi: (b, h))
    kr_blk = pl.BlockSpec((seq, LANES), lambda b, h, qi: (b, 0))
    meta_h = pl.BlockSpec((n_meta_pad, LANES), lambda b, h, qi: (0, h))
    meta_0 = pl.BlockSpec((n_meta_pad, LANES), lambda b, h, qi: (0, 0))
    bias_blk = pl.BlockSpec((1, n_meta_pad), lambda b, h, qi: (0, 0))
    return pl.pallas_call(
        functools.partial(_attn_kernel, tk=tk),
        grid=(batch, A_HEADS, nq),
        in_specs=[q_blk, q_blk, k_blk, kr_blk, k_blk, meta_h, meta_0, meta_h, bias_blk],
        out_specs=q_blk,
        out_shape=jax.ShapeDtypeStruct((rows, A_HEADS * A_DV), BF16),
        compiler_params=pltpu.CompilerParams(
            dimension_semantics=("parallel", "parallel", "parallel"),
            vmem_limit_bytes=VMEM_LIMIT),
        name="attention",
    )(qn, qr, kn, kr, va, knm, krm, vam, bias)


def _mix_kernel(x_ref, hf_ref, hb_ref, so_ref, att_ref, sg_ref, gm_ref, wm_ref, wa_ref, wo_ref,
                gffn_ref, wrh_ref, wrl_ref, br_ref, h1_ref, u2_ref, idx_ref, gate_ref, *, n_split):
    d = x_ref.shape[1]
    tr = x_ref.shape[0] // n_split
    for part in range(n_split):
        rows = slice(part * tr, (part + 1) * tr)
        hs = hf_ref[rows, :].astype(F32) + hb_ref[rows, :].astype(F32)
        parts = []
        for h in range(M_HEADS):
            blk = hs[:, h * M_DV:(h + 1) * M_DV]
            parts.append(blk * lax.rsqrt(jnp.mean(blk * blk, axis=-1, keepdims=True) + EPS))
        hn = jnp.concatenate(parts, axis=1) * gm_ref[...] * so_ref[rows, :].astype(F32)
        y_m = _mm(hn.astype(BF16), wm_ref[...])
        y_a = _mm(att_ref[rows, :], wa_ref[...])
        sg = sg_ref[rows, :].astype(F32)
        mixed = sg[:, :d] * y_m + sg[:, d:] * y_a
        h1 = x_ref[rows, :] + _mm(mixed.astype(BF16), wo_ref[...])
        h1_ref[rows, :] = h1
        u2 = _rms(h1, gffn_ref[...])
        u2_hi = u2.astype(BF16)
        u2_ref[rows, :] = u2_hi
        u2_lo = (u2 - u2_hi.astype(F32)).astype(BF16)
        logits = (_mm(u2_hi, wrh_ref[...]) + _mm(u2_hi, wrl_ref[...]) + _mm(u2_lo, wrh_ref[...])
                  + br_ref[...])
        lane = lax.broadcasted_iota(jnp.int32, logits.shape, 1).astype(F32)
        out_lane = lax.broadcasted_iota(jnp.int32, (tr, TOP_K), 1)
        work = logits
        top_idx = jnp.zeros((tr, TOP_K), F32)
        top_exp = jnp.zeros((tr, TOP_K), F32)
        v_max = None
        for k in range(TOP_K):
            v = jnp.max(work, axis=-1, keepdims=True)
            idx = jnp.min(jnp.where(work == v, lane, float(N_EXPERTS)), axis=-1, keepdims=True)
            work = jnp.where(lane == idx, -jnp.inf, work)
            v_max = v if k == 0 else v_max
            top_idx = jnp.where(out_lane == k, idx, top_idx)
            top_exp = jnp.where(out_lane == k, jnp.exp(v - v_max), top_exp)
        idx_ref[rows, :] = top_idx.astype(jnp.int32)
        gate_ref[rows, :] = top_exp / jnp.sum(top_exp, axis=-1, keepdims=True)


def _mix(x2d, hf, hb, so, att, sg, weights, tm):
    rows, d = x2d.shape
    row_blk = lambda w: pl.BlockSpec((tm, w), lambda i: (i, 0))
    in_specs = ([row_blk(d), row_blk(d), row_blk(d), row_blk(d), row_blk(d), row_blk(2 * d)]
                + [_resident(w.shape) for w in weights])
    return pl.pallas_call(
        functools.partial(_mix_kernel, n_split=2 if tm % 32 == 0 else 1),
        grid=(rows // tm,),
        in_specs=in_specs,
        out_specs=[row_blk(d), row_blk(d), row_blk(TOP_K), row_blk(TOP_K)],
        out_shape=[jax.ShapeDtypeStruct((rows, d), F32), jax.ShapeDtypeStruct((rows, d), BF16),
                   jax.ShapeDtypeStruct((rows, TOP_K), jnp.int32),
                   jax.ShapeDtypeStruct((rows, TOP_K), F32)],
        compiler_params=pltpu.CompilerParams(dimension_semantics=("parallel",),
                                             vmem_limit_bytes=VMEM_LIMIT),
        name="mix",
    )(x2d, hf, hb, so, att, sg, *weights)


def _expert_kernel(be_ref, nused_ref, x_ref, w1_ref, b1_ref, w2_ref, b2_ref, y_ref, w1b_scr, w2b_scr):
    i = pl.program_id(0)

    @pl.when(i < nused_ref[0])
    def _():
        @pl.when((i == 0) | (be_ref[i] != be_ref[jnp.maximum(i - 1, 0)]))
        def _cast():
            w1b_scr[...] = w1_ref[...].astype(BF16)
            w2b_scr[...] = w2_ref[...].astype(BF16)

        hid = _mm(x_ref[...], w1b_scr[...]) + b1_ref[...]
        glu = jnp.minimum(hid[:, :D_FF], SWIGLU_LIMIT)
        lin = jnp.clip(hid[:, D_FF:], -SWIGLU_LIMIT, SWIGLU_LIMIT)
        act = glu * _sigmoid(SWIGLU_ALPHA * glu) * (lin + 1.0)
        y_ref[...] = (_mm(act.astype(BF16), w2b_scr[...]) + b2_ref[...]).astype(y_ref.dtype)


def _experts(block_e, n_used, xg, w1, b1, w2, b2, tme):
    rows, d = xg.shape
    nb = rows // tme
    grid_spec = pltpu.PrefetchScalarGridSpec(
        num_scalar_prefetch=2,
        grid=(nb,),
        in_specs=[
            pl.BlockSpec((tme, d), lambda i, be, nu: (i, 0)),
            pl.BlockSpec((None, d, 2 * D_FF), lambda i, be, nu: (be[i], 0, 0)),
            pl.BlockSpec((None, 1, 2 * D_FF), lambda i, be, nu: (be[i], 0, 0)),
            pl.BlockSpec((None, D_FF, d), lambda i, be, nu: (be[i], 0, 0)),
            pl.BlockSpec((None, 1, d), lambda i, be, nu: (be[i], 0, 0)),
        ],
        out_specs=pl.BlockSpec((tme, d), lambda i, be, nu: (i, 0)),
        scratch_shapes=[pltpu.VMEM((d, 2 * D_FF), BF16), pltpu.VMEM((D_FF, d), BF16)],
    )
    return pl.pallas_call(
        _expert_kernel,
        grid_spec=grid_spec,
        out_shape=jax.ShapeDtypeStruct((rows, d), BF16),
        compiler_params=pltpu.CompilerParams(dimension_semantics=("arbitrary",),
                                             vmem_limit_bytes=VMEM_LIMIT),
        name="experts",
    )(block_e, n_used, xg, w1, b1, w2, b2)


def _final_kernel(h1_ref, yg_ref, gate_ref, g_ref, o_ref):
    gates = gate_ref[...]
    y = h1_ref[...]
    for k in range(TOP_K):
        y = y + yg_ref[k].astype(F32) * gates[:, k:k + 1]
    o_ref[...] = _rms(y, g_ref[...])


def _final(h1, yg, gates, g, tm):
    rows, d = h1.shape
    row_blk = pl.BlockSpec((tm, d), lambda i: (i, 0))
    return pl.pallas_call(
        _final_kernel,
        grid=(rows // tm,),
        in_specs=[row_blk, pl.BlockSpec((TOP_K, tm, d), lambda i: (0, i, 0)),
                  pl.BlockSpec((tm, TOP_K), lambda i: (i, 0)), pl.BlockSpec((1, d), lambda i: (0, 0))],
        out_specs=row_blk,
        out_shape=jax.ShapeDtypeStruct((rows, d), F32),
        compiler_params=pltpu.CompilerParams(dimension_semantics=("parallel",)),
        name="final",
    )(h1, yg, gates, g)


def _rope_tables(pos):
    half = A_ROPE // 2
    freqs = ROPE_THETA ** (-jnp.arange(half, dtype=F32) / half)
    ang = pos.astype(F32)[:, None] * freqs[None, :]
    zeros = jnp.zeros((pos.shape[0], LANES - A_ROPE), F32)
    cos, sin = jnp.cos(ang), jnp.sin(ang)
    return jnp.concatenate([cos, cos, zeros], axis=1), jnp.concatenate([sin, sin, zeros], axis=1)


def _rope_pair(w):
    half = A_ROPE // 2
    zeros = jnp.zeros(w.shape[:-1] + (LANES - A_ROPE,), w.dtype)
    a = jnp.concatenate([w, zeros], axis=-1)
    b = jnp.concatenate([-w[..., half:], w[..., :half], zeros], axis=-1)
    return a, b


def _pick_tile(n, prefs):
    for t in prefs:
        if n % t == 0:
            return t
    return n


def kernel(x, meta_tokens, norm_mix, w_in, m_b_i, m_b_f, m_norm, w_m_proj, a_w_uq, a_w_ukv,
           a_norm_q, a_norm_kv, w_a_proj, w_out, norm_ffn, w_router, b_router, w1, b1, w2, b2,
           norm_final):
    assert norm_mix.shape[0] == 1, "single-layer encoder"
    batch, seq, d = x.shape
    rows = batch * seq
    x2d = x.reshape(rows, d)

    sizes = (M_HEADS * M_DQK, M_HEADS * M_DQK, M_HEADS * M_DV, M_HEADS * M_DV, 2 * M_HEADS,
             2 * M_HEADS, Q_LORA, KV_LORA, A_ROPE, 2 * d)
    offs = np.cumsum((0,) + sizes)
    w_q, w_k, w_v, w_o, w_i, w_f, w_cq, w_ckv, w_kr, w_g = (
        w_in[0][:, offs[i]:offs[i + 1]] for i in range(len(sizes)))
    kr_a, kr_b = _rope_pair(w_kr)
    w_qvt = jnp.concatenate([w_q * (M_DQK ** -0.5), w_v], axis=1).T.astype(BF16)
    w_ko = jnp.concatenate([w_k, w_o], axis=1).astype(BF16)
    w_lat = jnp.concatenate([w_cq, w_ckv, kr_a, kr_b], axis=1).astype(BF16)
    w_if = jnp.concatenate([w_i, w_f], axis=1).astype(BF16)
    b_if = jnp.concatenate([m_b_i[0].reshape(-1), m_b_f[0].reshape(-1)]).astype(F32)
    uq_a, uq_b = _rope_pair(a_w_uq[0][:, :, A_NOPE:])
    w_uq = jnp.concatenate([a_w_uq[0][:, :, :A_NOPE].reshape(Q_LORA, -1),
                            uq_a.reshape(Q_LORA, -1), uq_b.reshape(Q_LORA, -1)], axis=1).astype(BF16)
    w_ukv = jnp.concatenate([a_w_ukv[0][:, :, :A_NOPE].reshape(KV_LORA, -1),
                             a_w_ukv[0][:, :, A_NOPE:].reshape(KV_LORA, -1)], axis=1).astype(BF16)
    in_weights = (norm_mix[0].reshape(1, d), w_qvt, w_ko, w_g.astype(BF16), w_lat,
                  w_if, w_if.T, b_if.reshape(1, -1), b_if.reshape(-1, 1),
                  a_norm_q[0].reshape(1, -1), a_norm_kv[0].reshape(1, -1), w_uq, w_ukv)

    cos_r, sin_r = _rope_tables(jnp.arange(N_META, N_META + seq))
    cos_m, sin_m = _rope_tables(jnp.arange(N_META))
    tm = _pick_tile(seq, (256, 128, 64, 32, 16))
    (mq, mk, mv, so, sg, gl, glt, qn, qr, kn, va, kr) = _in_proj(x2d, cos_r, sin_r, in_weights, tm)
    (_, mk_m, mvt_m, _, _, _, glt_m, _, _, kn_m, va_m, kr_m) = _in_proj(
        meta_tokens.astype(x.dtype), cos_m, sin_m, in_weights, N_META)

    n_pad = LANES - N_META
    pad_gate = jnp.concatenate([jnp.full((n_pad, 2 * M_HEADS), NEG, F32),
                                jnp.zeros((n_pad, 2 * M_HEADS), F32)], axis=1)
    front = lambda a: jnp.concatenate([jnp.zeros((n_pad, a.shape[1]), a.dtype), a], axis=0)
    mvt_m_pad = jnp.concatenate([jnp.zeros((mvt_m.shape[0], n_pad), mvt_m.dtype), mvt_m], axis=1)
    mlstm_meta = (front(mk_m), mvt_m_pad, jnp.concatenate([pad_gate.T, glt_m], axis=1))
    lc = _pick_tile(seq, (256, 128))
    hf, hb = _mlstm(mq, mk, mv, gl, glt, mlstm_meta, batch, lc)

    back = lambda a: jnp.concatenate([a, jnp.zeros((n_pad, a.shape[1]), a.dtype)], axis=0)
    bias = jnp.concatenate([jnp.zeros((1, N_META), F32), jnp.full((1, n_pad), NEG, F32)], axis=1)
    tq = _pick_tile(seq, (512, 256, 128))
    tk = _pick_tile(seq, (512, 256, 128))
    att = _attention(qn, qr, kn, kr, va, (back(kn_m), back(kr_m), back(va_m), bias), batch, tq, tk)

    wr_hi = w_router[0].astype(BF16)
    wr_lo = (w_router[0] - wr_hi.astype(F32)).astype(BF16)
    mix_weights = (m_norm[0].reshape(1, -1), w_m_proj[0].astype(BF16), w_a_proj[0].astype(BF16),
                   w_out[0].astype(BF16), norm_ffn[0].reshape(1, d), wr_hi, wr_lo,
                   b_router[0].reshape(1, -1))
    h1, u2, top_idx, gates = _mix(x2d, hf, hb, so, att, sg, mix_weights,
                                  _pick_tile(rows, (512, 256, 128, 64, 32, 16)))

    tme = 512
    n = rows * TOP_K
    nb = -(-n // tme) + N_EXPERTS
    i32 = jnp.int32
    flat_e = top_idx.reshape(-1)
    pair_ids = jnp.arange(n, dtype=i32)
    _, order = lax.sort((flat_e, pair_ids), num_keys=1, is_stable=True)
    _, rank = lax.sort((order, pair_ids), num_keys=1)
    counts = jnp.sum(flat_e[:, None] == jnp.arange(N_EXPERTS, dtype=i32)[None, :], axis=0, dtype=i32)
    start = jnp.cumsum(counts) - counts
    padded = (counts + tme - 1) // tme * tme
    padded_end = jnp.cumsum(padded)
    padded_start = padded_end - padded
    block_row0 = jnp.arange(nb, dtype=i32) * tme
    block_e = jnp.minimum(jnp.sum(padded_end[None, :] <= block_row0[:, None], axis=1, dtype=i32),
                          N_EXPERTS - 1)
    n_used = (padded_end[-1] // tme).astype(i32).reshape(1)
    off = (block_row0 - padded_start[block_e])[:, None] + jnp.arange(tme, dtype=i32)[None, :]
    valid = off < counts[block_e][:, None]
    src = jnp.clip(start[block_e][:, None] + off, 0, n - 1)
    filler = (block_row0[:, None] + jnp.arange(tme, dtype=i32)[None, :]) % rows
    buf_tok = jnp.where(valid, order[src] // TOP_K, filler).reshape(-1)
    slot = padded_start[flat_e] + rank - start[flat_e]
    xg = u2[buf_tok]
    yb = _experts(block_e, n_used, xg, w1[0], b1[0].reshape(N_EXPERTS, 1, -1), w2[0],
                  b2[0].reshape(N_EXPERTS, 1, -1), tme)
    yg = yb[slot.reshape(rows, TOP_K).T.reshape(-1)].reshape(TOP_K, rows, d)

    out = _final(h1, yg, gates, norm_final.reshape(1, d), tm)
    return out.reshape(batch, seq, d)
```

```python
import functools

import jax
import jax.numpy as jnp
import numpy as np
from jax import lax
from jax.experimental import pallas as pl
from jax.experimental.pallas import tpu as pltpu

N_META = 16
EPS = 1e-6
NEG = -1e30
M_HEADS = 8
M_DQK = 64
M_DV = 128
A_HEADS = 8
A_NOPE = 128
A_ROPE = 64
A_DV = 128
Q_LORA = 256
KV_LORA = 128
ROPE_THETA = 10000.0
N_EXPERTS = 32
TOP_K = 4
D_FF = 1024
SWIGLU_ALPHA = 1.702
SWIGLU_LIMIT = 7.0

LANES = 128
ATT_Q_TILES = (1024, 512, 256, 128)
ONES_ROWS = 16
N_GATE_COLS = 4 * M_HEADS
ATT_SCALE = (A_NOPE + A_ROPE) ** -0.5 * 1.4426950408889634
VMEM_LIMIT = 56 * 1024 * 1024

F32 = jnp.float32
BF16 = jnp.bfloat16


def _mm(a, b):
    return jnp.dot(a, b, preferred_element_type=F32)


def _mm_nt(a, b):
    return lax.dot_general(a, b, (((1,), (1,)), ((), ())), preferred_element_type=F32)


def _mm_tn(a, b):
    return lax.dot_general(a, b, (((0,), (0,)), ((), ())), preferred_element_type=F32)


def _mm_exact(a, b):
    return jnp.dot(a, b, preferred_element_type=F32, precision=lax.Precision.HIGHEST)


def _rms(x, g):
    return x * lax.rsqrt(jnp.mean(x * x, axis=-1, keepdims=True) + EPS) * g


def _log_sigmoid(x):
    return jnp.minimum(x, 0.0) - jnp.log1p(jnp.exp(-jnp.abs(x)))


def _sigmoid(x):
    return 1.0 / (1.0 + jnp.exp(-x))


def _resident(shape):
    nd = len(shape)
    return pl.BlockSpec(shape, lambda *_: (0,) * nd, pipeline_mode=pl.Buffered(1))


def _in_proj_kernel(x_ref, cos_ref, sin_ref, cost_ref, sint_ref, gmix_ref, wqvt_ref, wko_ref, wgate_ref,
                    wlat_ref, wif_ref, wift_ref, bif_ref, bift_ref, gq_ref, gkv_ref, wuqt_ref, wuk_ref,
                    wuvt_ref, mqt_ref, mk_ref, mvt_ref, so_ref, sg_ref, gl_ref, glt_ref,
                    qnt_ref, qrt_ref, kn_ref, vat_ref, kr_ref):
    d_qk = M_HEADS * M_DQK
    ub = _rms(x_ref[...], gmix_ref[...]).astype(BF16)

    qvt = _mm_nt(wqvt_ref[...], ub)
    mqt_ref[...] = qvt[:d_qk, :].astype(BF16)
    mvt_ref[...] = qvt[d_qk:, :].astype(BF16)
    ko = _mm(ub, wko_ref[...])
    mk_ref[...] = ko[:, :d_qk].astype(BF16)
    so_ref[...] = _sigmoid(ko[:, d_qk:]).astype(BF16)
    sg_ref[...] = _sigmoid(_mm(ub, wgate_ref[...])).astype(BF16)

    gi = _mm(ub, wif_ref[...]) + bif_ref[...]
    col = lax.broadcasted_iota(jnp.int32, gi.shape, 1)
    gl_ref[...] = jnp.where(col < 2 * M_HEADS, gi, _log_sigmoid(gi))
    git = _mm_nt(wift_ref[...], ub) + bift_ref[...]
    row = lax.broadcasted_iota(jnp.int32, git.shape, 0)
    glt_ref[...] = jnp.where(row < 2 * M_HEADS, git, _log_sigmoid(git))

    lat = _mm(ub, wlat_ref[...])
    cos = cos_ref[...]
    sin = sin_ref[...]
    kr_ref[...] = (lat[:, Q_LORA + KV_LORA:Q_LORA + KV_LORA + LANES] * cos
                   + lat[:, Q_LORA + KV_LORA + LANES:] * sin).astype(BF16)

    cqn = _rms(lat[:, :Q_LORA], gq_ref[...]).astype(BF16)
    q3t = _mm_nt(wuqt_ref[...], cqn)
    d_n = A_HEADS * A_NOPE
    qnt_ref[...] = (q3t[:d_n, :] * ATT_SCALE).astype(BF16)
    cos_t = cost_ref[...]
    sin_t = sint_ref[...]
    for h in range(A_HEADS):
        a = q3t[d_n + h * LANES:d_n + (h + 1) * LANES, :]
        b = q3t[d_n + (A_HEADS + h) * LANES:d_n + (A_HEADS + h + 1) * LANES, :]
        qrt_ref[h * LANES:(h + 1) * LANES, :] = ((a * cos_t + b * sin_t) * ATT_SCALE).astype(BF16)

    ckvn = _rms(lat[:, Q_LORA:Q_LORA + KV_LORA], gkv_ref[...]).astype(BF16)
    kn_ref[...] = _mm(ckvn, wuk_ref[...]).astype(BF16)
    vat_ref[...] = _mm_nt(wuvt_ref[...], ckvn).astype(BF16)


def _in_proj(x2d, cos4, sin4, weights, tm):
    rows, d = x2d.shape
    n_tab = cos4.shape[0] // tm
    row_blk = lambda w: pl.BlockSpec((tm, w), lambda i: (i, 0))
    tab_blk = pl.BlockSpec((tm, LANES), lambda i: (i % n_tab, 0))
    tab_t_blk = pl.BlockSpec((LANES, tm), lambda i: (0, i % n_tab))
    d_qk, d_v, d_n = M_HEADS * M_DQK, M_HEADS * M_DV, A_HEADS * A_NOPE
    out_shape = [
        jax.ShapeDtypeStruct((d_qk, rows), BF16),
        jax.ShapeDtypeStruct((rows, d_qk), BF16),
        jax.ShapeDtypeStruct((d_v, rows), BF16),
        jax.ShapeDtypeStruct((rows, d_v), BF16),
        jax.ShapeDtypeStruct((rows, 2 * d), BF16),
        jax.ShapeDtypeStruct((rows, N_GATE_COLS), F32),
        jax.ShapeDtypeStruct((N_GATE_COLS, rows), F32),
        jax.ShapeDtypeStruct((d_n, rows), BF16),
        jax.ShapeDtypeStruct((A_HEADS * LANES, rows), BF16),
        jax.ShapeDtypeStruct((rows, d_n), BF16),
        jax.ShapeDtypeStruct((A_HEADS * A_DV, rows), BF16),
        jax.ShapeDtypeStruct((rows, LANES), BF16),
    ]
    col_blk = lambda h: pl.BlockSpec((h, tm), lambda i: (0, i))
    out_specs = [col_blk(d_qk), row_blk(d_qk), col_blk(d_v), row_blk(d_v), row_blk(2 * d),
                 row_blk(N_GATE_COLS), col_blk(N_GATE_COLS),
                 col_blk(d_n), col_blk(A_HEADS * LANES), row_blk(d_n), col_blk(A_HEADS * A_DV),
                 row_blk(LANES)]
    in_specs = ([row_blk(d), tab_blk, tab_blk, tab_t_blk, tab_t_blk]
                + [_resident(w.shape) for w in weights])
    return pl.pallas_call(
        _in_proj_kernel,
        grid=(rows // tm,),
        in_specs=in_specs,
        out_specs=out_specs,
        out_shape=out_shape,
        compiler_params=pltpu.CompilerParams(dimension_semantics=("parallel",),
                                             vmem_limit_bytes=VMEM_LIMIT),
        name="in_proj",
    )(x2d, cos4, sin4, cos4.T, sin4.T, *weights)


def _mlstm_kernel(qf_ref, kf_ref, vf_ref, gf_ref, gtf_ref,
                  qb_ref, kb_ref, vb_ref, gb_ref, gtb_ref,
                  km_ref, vm_ref, gtm_ref,
                  hf_ref, hb_ref, c_scr, m_scr):
    c_idx = pl.program_id(1)
    lc = kf_ref.shape[0]
    lm = km_ref.shape[0]
    n_pairs = M_HEADS // 2
    half = LANES // 2
    log2e = 1.4426950408889634

    def tri(n, lower):
        r = lax.broadcasted_iota(jnp.int32, (n, n), 0)
        c = lax.broadcasted_iota(jnp.int32, (n, n), 1)
        return (r >= c) if lower else (r <= c)

    def as_bf16(mask):
        return jnp.where(mask, 1.0, 0.0).astype(BF16)

    def eye(n):
        r = lax.broadcasted_iota(jnp.int32, (n, n), 0)
        c = lax.broadcasted_iota(jnp.int32, (n, n), 1)
        return as_bf16(r == c)

    eye_t = eye(lc)
    lane = lax.broadcasted_iota(jnp.int32, (1, LANES), 1)
    sub128 = lax.broadcasted_iota(jnp.int32, (LANES, 1), 0)

    def lane_mask(h):
        return (lane < half) if h % 2 == 0 else (lane >= half)

    def sublane_mask(h):
        return (sub128 < half) if h % 2 == 0 else (sub128 >= half)

    def split3(x):
        hi = x.astype(BF16)
        r1 = x - hi.astype(F32)
        mid = r1.astype(BF16)
        lo = (r1 - mid.astype(F32)).astype(BF16)
        return hi, mid, lo

    def cum_rows(g_rows, tri_b):
        hi, mid, lo = split3(g_rows)
        return _mm(hi, tri_b) + _mm(mid, tri_b) + _mm(lo, tri_b)

    def cum_cols(tri_b, g_cols):
        hi, mid, lo = split3(g_cols)
        return _mm(tri_b, hi) + _mm(tri_b, mid) + _mm(tri_b, lo)

    def state_update(direction, pair, k2, vt_pair, cum_row, g_rows, tot_col):
        n = k2.shape[0]
        d_c = None
        decays = []
        for sub in range(2):
            h = 2 * pair + sub
            j = direction * M_HEADS + h
            jf = 2 * M_HEADS + j
            m_prev = m_scr[j][0:1, 0:1]
            tot = tot_col[jf:jf + 1, :]
            g_row = tot - cum_row[jf:jf + 1, :] + g_rows[j:j + 1, :]
            m_new = jnp.maximum(tot + m_prev, jnp.max(g_row, axis=1, keepdims=True))
            decays.append(jnp.exp2(tot + m_prev - m_new))
            wk = jnp.exp2(g_row - m_new)
            vw = jnp.concatenate([(vt_pair[sub].astype(F32) * wk).astype(BF16),
                                  jnp.broadcast_to(wk.astype(BF16), (ONES_ROWS, n))], axis=0)
            km = jnp.where(lane_mask(h), k2, jnp.zeros_like(k2))
            upd = _mm(vw, km)
            d_c = upd if d_c is None else d_c + upd
            m_scr[j] = jnp.broadcast_to(m_new, m_scr.shape[1:])
        decay_cols = jnp.where(lane < half, decays[0], decays[1])
        ci = direction * n_pairs + pair
        c_scr[ci] = decay_cols * c_scr[ci] + d_c

    @pl.when(c_idx == 0)
    def _init():
        c_scr[...] = jnp.zeros_like(c_scr)
        m_scr[...] = jnp.zeros_like(m_scr)
        g_rows = gtm_ref[...] * log2e
        cum_m = cum_rows(g_rows, as_bf16(tri(lm, False)))
        tot_m = cum_m[:, lm - 1:lm]
        for pair in range(n_pairs):
            k2 = km_ref[:, pair * LANES:(pair + 1) * LANES]
            vt_pair = [vm_ref[(2 * pair + s) * M_DV:(2 * pair + s + 1) * M_DV, :] for s in range(2)]
            state_update(0, pair, k2, vt_pair, cum_m, g_rows, tot_m)

    views = ((qf_ref, kf_ref, vf_ref, gf_ref, gtf_ref, hf_ref),
             (qb_ref, kb_ref, vb_ref, gb_ref, gtb_ref, hb_ref))
    shared = []
    for direction, (_, _, _, g_ref, gt_ref, _) in enumerate(views):
        fwd = direction == 0
        g_cols = g_ref[...] * log2e
        g_rows = gt_ref[...] * log2e
        cum_row = cum_rows(g_rows, as_bf16(tri(lc, not fwd)))
        cum_col = cum_cols(as_bf16(tri(lc, fwd)), g_cols)
        tot_col = cum_row[:, lc - 1:lc] if fwd else cum_row[:, 0:1]
        key_cols = g_cols[:, :2 * M_HEADS] - cum_col[:, 2 * M_HEADS:]
        valid = tri(lc, not fwd)
        shared.append((g_rows, cum_row, tot_col, key_cols, valid))

    units = [(d, p, s) for d in range(2) for p in range(n_pairs) for s in range(2)]
    pair_vals = {}
    vals = {}

    def scores(u):
        direction, pair, sub = units[u]
        q_ref, k_ref, v_ref = views[direction][:3]
        h = 2 * pair + sub
        if sub == 0:
            q2t = q_ref[pair * LANES:(pair + 1) * LANES, :]
            k2 = k_ref[:, pair * LANES:(pair + 1) * LANES]
            c_b = c_scr[direction * n_pairs + pair].astype(BF16)
            pair_vals[(direction, pair)] = (q2t, k2, c_b, [])
        q2t, k2, _, vt_pair = pair_vals[(direction, pair)]
        vt = v_ref[h * M_DV:(h + 1) * M_DV, :]
        vt_pair.append(vt)
        qmt = jnp.where(sublane_mask(h), q2t, jnp.zeros_like(q2t))
        vals[u] = dict(vt=vt, qmt=qmt, st=_mm(k2, qmt))

    def accumulate(u):
        direction, pair, sub = units[u]
        g_rows, cum_row, tot_col, key_cols, valid = shared[direction]
        _, k2, c_b, vt_pair = pair_vals[(direction, pair)]
        j = direction * M_HEADS + 2 * pair + sub
        jf = 2 * M_HEADS + j
        v = vals[u]
        b_row = cum_row[jf:jf + 1, :]
        dt = jnp.where(valid, key_cols[:, j:j + 1] + b_row, NEG)
        m_inter = b_row + m_scr[j][0:1, 0:1]
        m_t = jnp.maximum(m_inter, jnp.max(dt, axis=0, keepdims=True))
        wt = (jnp.exp2(dt - m_t) * v['st']).astype(BF16)
        qs = (v['qmt'].astype(F32) * jnp.exp2(m_inter - m_t)).astype(BF16)
        vext = jnp.concatenate([v['vt'], jnp.ones((ONES_ROWS, lc), BF16)], axis=0)
        v['acc'] = _mm(vext, wt) + _mm(c_b, qs)
        v['m_t'] = m_t
        if sub == 1:
            state_update(direction, pair, k2, vt_pair, cum_row, g_rows, tot_col)

    def finish(u):
        direction, pair, sub = units[u]
        h_ref = views[direction][5]
        h = 2 * pair + sub
        v = vals.pop(u)
        den = jnp.maximum(jnp.abs(v['acc'][M_DV:M_DV + 1, :]), jnp.exp2(-v['m_t']))
        ht = (v['acc'][:M_DV, :] * (1.0 / den)).astype(BF16)
        h_ref[:, h * M_DV:(h + 1) * M_DV] = _mm_nt(eye_t, ht).astype(h_ref.dtype)

    for t in range(len(units) + 2):
        if t < len(units):
            scores(t)
        if 0 <= t - 1 < len(units):
            accumulate(t - 1)
        if 0 <= t - 2 < len(units):
            finish(t - 2)


def _mlstm(mq, mk, mv, gl, glt, meta, batch, lc):
    rows = mk.shape[0]
    seq = rows // batch
    nc = seq // lc
    km, vm, gtm = meta
    d_qk, d_v = M_HEADS * M_DQK, M_HEADS * M_DV

    fwd_row = lambda b, c: (b * nc + c, 0)
    bwd_row = lambda b, c: (b * nc + nc - 1 - c, 0)
    fwd_col = lambda b, c: (0, b * nc + c)
    bwd_col = lambda b, c: (0, b * nc + nc - 1 - c)

    def views(row_map, col_map):
        return [pl.BlockSpec((d_qk, lc), col_map), pl.BlockSpec((lc, d_qk), row_map),
                pl.BlockSpec((d_v, lc), col_map), pl.BlockSpec((lc, N_GATE_COLS), row_map),
                pl.BlockSpec((N_GATE_COLS, lc), col_map)]

    const = lambda a: pl.BlockSpec(a.shape, lambda b, c: (0, 0))
    in_specs = views(fwd_row, fwd_col) + views(bwd_row, bwd_col) + [const(a) for a in meta]
    return pl.pallas_call(
        _mlstm_kernel,
        grid=(batch, nc),
        in_specs=in_specs,
        out_specs=[pl.BlockSpec((lc, d_v), fwd_row), pl.BlockSpec((lc, d_v), bwd_row)],
        out_shape=[jax.ShapeDtypeStruct((rows, d_v), BF16)] * 2,
        scratch_shapes=[pltpu.VMEM((2 * (M_HEADS // 2), M_DV + ONES_ROWS, LANES), F32),
                        pltpu.VMEM((2 * M_HEADS, 8, LANES), F32)],
        compiler_params=pltpu.CompilerParams(dimension_semantics=("parallel", "arbitrary"),
                                             vmem_limit_bytes=VMEM_LIMIT),
        name="mlstm",
    )(mq, mk, mv, gl, glt, mq, mk, mv, gl, glt, km, vm, gtm)


def _attn_kernel(qnt_ref, qrt_ref, kn_ref, kr_ref, vt_ref, knm_ref, krm_ref, vtm_ref, o_ref, *, tk):
    qt = jnp.concatenate([qnt_ref[...], qrt_ref[...]], axis=0)

    def vext(vt):
        return jnp.concatenate([vt, jnp.ones((ONES_ROWS, vt.shape[1]), BF16)], axis=0)

    km = jnp.concatenate([knm_ref[...], krm_ref[...]], axis=1)
    s = _mm(km, qt)
    key = lax.broadcasted_iota(jnp.int32, s.shape, 0)
    s = jnp.where(key < N_META, s, NEG)
    m = jnp.max(s, axis=0, keepdims=True)
    acc = _mm(vext(vtm_ref[...]), jnp.exp2(s - m).astype(BF16))
    n_chunks = kn_ref.shape[0] // tk

    def scores(j):
        rows = slice(j * tk, (j + 1) * tk)
        k = jnp.concatenate([kn_ref[rows, :], kr_ref[rows, :]], axis=1)
        return _mm(k, qt)

    s_next = scores(0)
    for j in range(n_chunks):
        s = s_next
        if j + 1 < n_chunks:
            s_next = scores(j + 1)
        m_new = jnp.maximum(m, jnp.max(s, axis=0, keepdims=True))
        p = jnp.exp2(s - m_new).astype(BF16)
        acc = jnp.exp2(m - m_new) * acc + _mm(vext(vt_ref[:, j * tk:(j + 1) * tk]), p)
        m = m_new
    out_t = acc[:A_DV, :] * (1.0 / acc[A_DV:A_DV + 1, :])
    o_ref[...] = out_t.T.astype(o_ref.dtype)


def _attention(qnt, qrt, kn, kr, vat, meta, batch, tq, tk):
    rows = kn.shape[0]
    seq = rows // batch
    nq = seq // tq
    knm, krm, vatm = meta
    n_meta_pad = knm.shape[0]
    qt_blk = pl.BlockSpec((LANES, tq), lambda b, h, qi: (h, b * nq + qi))
    k_blk = pl.BlockSpec((seq, LANES), lambda b, h, qi: (b, h))
    kr_blk = pl.BlockSpec((seq, LANES), lambda b, h, qi: (b, 0))
    vt_blk = pl.BlockSpec((A_DV, seq), lambda b, h, qi: (h, b))
    meta_h = pl.BlockSpec((n_meta_pad, LANES), lambda b, h, qi: (0, h))
    meta_0 = pl.BlockSpec((n_meta_pad, LANES), lambda b, h, qi: (0, 0))
    meta_vt = pl.BlockSpec((A_DV, n_meta_pad), lambda b, h, qi: (h, 0))
    return pl.pallas_call(
        functools.partial(_attn_kernel, tk=tk),
        grid=(batch, A_HEADS, nq),
        in_specs=[qt_blk, qt_blk, k_blk, kr_blk, vt_blk, meta_h, meta_0, meta_vt],
        out_specs=pl.BlockSpec((tq, A_DV), lambda b, h, qi: (b * nq + qi, h)),
        out_shape=jax.ShapeDtypeStruct((rows, A_HEADS * A_DV), BF16),
        compiler_params=pltpu.CompilerParams(
            dimension_semantics=("parallel", "parallel", "parallel"),
            vmem_limit_bytes=VMEM_LIMIT),
        name="attention",
    )(qnt, qrt, kn, kr, vat, knm, krm, vatm)


def _mix_kernel(x_ref, hf_ref, hb_ref, so_ref, att_ref, sg_ref, gm_ref, wm_ref, wa_ref, wo_ref,
                gffn_ref, wrh_ref, wrl_ref, br_ref, h1_ref, u2_ref, idx_ref, gate_ref, *, n_split):
    d = x_ref.shape[1]
    tr = x_ref.shape[0] // n_split
    for part in range(n_split):
        rows = slice(part * tr, (part + 1) * tr)
        hs = hf_ref[rows, :].astype(F32) + hb_ref[rows, :].astype(F32)
        parts = []
        for h in range(M_HEADS):
            blk = hs[:, h * M_DV:(h + 1) * M_DV]
            parts.append(blk * lax.rsqrt(jnp.mean(blk * blk, axis=-1, keepdims=True) + EPS))
        hn = jnp.concatenate(parts, axis=1) * gm_ref[...] * so_ref[rows, :].astype(F32)
        y_m = _mm(hn.astype(BF16), wm_ref[...])
        y_a = _mm(att_ref[rows, :], wa_ref[...])
        sg = sg_ref[rows, :].astype(F32)
        mixed = sg[:, :d] * y_m + sg[:, d:] * y_a
        h1 = x_ref[rows, :] + _mm(mixed.astype(BF16), wo_ref[...])
        h1_ref[rows, :] = h1
        u2 = _rms(h1, gffn_ref[...])
        u2_hi = u2.astype(BF16)
        u2_ref[rows, :] = u2_hi
        u2_lo = (u2 - u2_hi.astype(F32)).astype(BF16)
        logits = (_mm(u2_hi, wrh_ref[...]) + _mm(u2_hi, wrl_ref[...]) + _mm(u2_lo, wrh_ref[...])
                  + br_ref[...])
        lane = lax.broadcasted_iota(jnp.int32, logits.shape, 1).astype(F32)
        out_lane = lax.broadcasted_iota(jnp.int32, (tr, TOP_K), 1)
        work = logits
        top_idx = jnp.zeros((tr, TOP_K), F32)
        top_exp = jnp.zeros((tr, TOP_K), F32)
        v_max = None
        for k in range(TOP_K):
            v = jnp.max(work, axis=-1, keepdims=True)
            idx = jnp.min(jnp.where(work == v, lane, float(N_EXPERTS)), axis=-1, keepdims=True)
            work = jnp.where(lane == idx, -jnp.inf, work)
            v_max = v if k == 0 else v_max
            top_idx = jnp.where(out_lane == k, idx, top_idx)
            top_exp = jnp.where(out_lane == k, jnp.exp(v - v_max), top_exp)
        idx_ref[rows, :] = top_idx.astype(jnp.int32)
        gate_ref[rows, :] = top_exp / jnp.sum(top_exp, axis=-1, keepdims=True)


def _mix(x2d, hf, hb, so, att, sg, weights, tm):
    rows, d = x2d.shape
    row_blk = lambda w: pl.BlockSpec((tm, w), lambda i: (i, 0))
    in_specs = ([row_blk(d), row_blk(d), row_blk(d), row_blk(d), row_blk(d), row_blk(2 * d)]
                + [_resident(w.shape) for w in weights])
    return pl.pallas_call(
        functools.partial(_mix_kernel, n_split=2 if tm % 32 == 0 else 1),
        grid=(rows // tm,),
        in_specs=in_specs,
        out_specs=[row_blk(d), row_blk(d), row_blk(TOP_K), row_blk(TOP_K)],
        out_shape=[jax.ShapeDtypeStruct((rows, d), F32), jax.ShapeDtypeStruct((rows, d), BF16),
                   jax.ShapeDtypeStruct((rows, TOP_K), jnp.int32),
                   jax.ShapeDtypeStruct((rows, TOP_K), F32)],
        compiler_params=pltpu.CompilerParams(dimension_semantics=("parallel",),
                                             vmem_limit_bytes=VMEM_LIMIT),
        name="mix",
    )(x2d, hf, hb, so, att, sg, *weights)


def _expert_kernel(be_ref, nused_ref, x_ref, w1_ref, b1_ref, w2_ref, b2_ref, y_ref, w1b_scr, w2b_scr):
    i = pl.program_id(0)

    @pl.when(i < nused_ref[0])
    def _():
        @pl.when((i == 0) | (be_ref[i] != be_ref[jnp.maximum(i - 1, 0)]))
        def _cast():
            w1b_scr[...] = w1_ref[...].astype(BF16)
            w2b_scr[...] = w2_ref[...].astype(BF16)

        hid = _mm(x_ref[...], w1b_scr[...]) + b1_ref[...]
        glu = jnp.minimum(hid[:, :D_FF], SWIGLU_LIMIT)
        lin = jnp.clip(hid[:, D_FF:], -SWIGLU_LIMIT, SWIGLU_LIMIT)
        act = glu * _sigmoid(SWIGLU_ALPHA * glu) * (lin + 1.0)
        y_ref[...] = (_mm(act.astype(BF16), w2b_scr[...]) + b2_ref[...]).astype(y_ref.dtype)

    @pl.when(i >= nused_ref[0])
    def _():
        y_ref[...] = jnp.zeros_like(y_ref)


def _experts(block_e, n_used, xg, w1, b1, w2, b2, tme):
    rows, d = xg.shape
    nb = rows // tme
    grid_spec = pltpu.PrefetchScalarGridSpec(
        num_scalar_prefetch=2,
        grid=(nb,),
        in_specs=[
            pl.BlockSpec((tme, d), lambda i, be, nu: (i, 0)),
            pl.BlockSpec((None, d, 2 * D_FF), lambda i, be, nu: (be[i], 0, 0)),
            pl.BlockSpec((None, 1, 2 * D_FF), lambda i, be, nu: (be[i], 0, 0)),
            pl.BlockSpec((None, D_FF, d), lambda i, be, nu: (be[i], 0, 0)),
            pl.BlockSpec((None, 1, d), lambda i, be, nu: (be[i], 0, 0)),
        ],
        out_specs=pl.BlockSpec((tme, d), lambda i, be, nu: (i, 0)),
        scratch_shapes=[pltpu.VMEM((d, 2 * D_FF), BF16), pltpu.VMEM((D_FF, d), BF16)],
    )
    return pl.pallas_call(
        _expert_kernel,
        grid_spec=grid_spec,
        out_shape=jax.ShapeDtypeStruct((rows, d), BF16),
        compiler_params=pltpu.CompilerParams(dimension_semantics=("arbitrary",),
                                             vmem_limit_bytes=VMEM_LIMIT),
        name="experts",
    )(block_e, n_used, xg, w1, b1, w2, b2)


def _final_kernel(h1_ref, yg_ref, gate_ref, g_ref, o_ref):
    gates = gate_ref[...]
    y = h1_ref[...]
    for k in range(TOP_K):
        y = y + yg_ref[k].astype(F32) * gates[:, k:k + 1]
    o_ref[...] = _rms(y, g_ref[...])


def _final(h1, yg, gates, g, tm):
    rows, d = h1.shape
    row_blk = pl.BlockSpec((tm, d), lambda i: (i, 0))
    return pl.pallas_call(
        _final_kernel,
        grid=(rows // tm,),
        in_specs=[row_blk, pl.BlockSpec((TOP_K, tm, d), lambda i: (0, i, 0)),
                  pl.BlockSpec((tm, TOP_K), lambda i: (i, 0)), pl.BlockSpec((1, d), lambda i: (0, 0))],
        out_specs=row_blk,
        out_shape=jax.ShapeDtypeStruct((rows, d), F32),
        compiler_params=pltpu.CompilerParams(dimension_semantics=("parallel",)),
        name="final",
    )(h1, yg, gates, g)


def _rope_tables(pos):
    half = A_ROPE // 2
    freqs = ROPE_THETA ** (-jnp.arange(half, dtype=F32) / half)
    ang = pos.astype(F32)[:, None] * freqs[None, :]
    zeros = jnp.zeros((pos.shape[0], LANES - A_ROPE), F32)
    cos, sin = jnp.cos(ang), jnp.sin(ang)
    return jnp.concatenate([cos, cos, zeros], axis=1), jnp.concatenate([sin, sin, zeros], axis=1)


def _rope_pair(w):
    half = A_ROPE // 2
    zeros = jnp.zeros(w.shape[:-1] + (LANES - A_ROPE,), w.dtype)
    a = jnp.concatenate([w, zeros], axis=-1)
    b = jnp.concatenate([-w[..., half:], w[..., :half], zeros], axis=-1)
    return a, b


def _pick_tile(n, prefs):
    for t in prefs:
        if n % t == 0:
            return t
    return n


def kernel(x, meta_tokens, norm_mix, w_in, m_b_i, m_b_f, m_norm, w_m_proj, a_w_uq, a_w_ukv,
           a_norm_q, a_norm_kv, w_a_proj, w_out, norm_ffn, w_router, b_router, w1, b1, w2, b2,
           norm_final):
    assert norm_mix.shape[0] == 1, "single-layer encoder"
    batch, seq, d = x.shape
    rows = batch * seq
    x2d = x.reshape(rows, d)

    sizes = (M_HEADS * M_DQK, M_HEADS * M_DQK, M_HEADS * M_DV, M_HEADS * M_DV, 2 * M_HEADS,
             2 * M_HEADS, Q_LORA, KV_LORA, A_ROPE, 2 * d)
    offs = np.cumsum((0,) + sizes)
    w_q, w_k, w_v, w_o, w_i, w_f, w_cq, w_ckv, w_kr, w_g = (
        w_in[0][:, offs[i]:offs[i + 1]] for i in range(len(sizes)))
    kr_a, kr_b = _rope_pair(w_kr)
    w_qvt = jnp.concatenate([w_q * (M_DQK ** -0.5), w_v], axis=1).T.astype(BF16)
    w_ko = jnp.concatenate([w_k, w_o], axis=1).astype(BF16)
    w_lat = jnp.concatenate([w_cq, w_ckv, kr_a, kr_b], axis=1).astype(BF16)
    w_if = jnp.concatenate([w_i, w_f], axis=1).astype(BF16)
    b_if = jnp.concatenate([m_b_i[0].reshape(-1), m_b_f[0].reshape(-1)]).astype(F32)
    uq_a, uq_b = _rope_pair(a_w_uq[0][:, :, A_NOPE:])
    w_uqt = jnp.concatenate([a_w_uq[0][:, :, :A_NOPE].reshape(Q_LORA, -1), uq_a.reshape(Q_LORA, -1),
                             uq_b.reshape(Q_LORA, -1)], axis=1).T.astype(BF16)
    w_uk = a_w_ukv[0][:, :, :A_NOPE].reshape(KV_LORA, -1).astype(BF16)
    w_uvt = a_w_ukv[0][:, :, A_NOPE:].reshape(KV_LORA, -1).T.astype(BF16)
    in_weights = (norm_mix[0].reshape(1, d), w_qvt, w_ko, w_g.astype(BF16), w_lat,
                  w_if, w_if.T, b_if.reshape(1, -1), b_if.reshape(-1, 1),
                  a_norm_q[0].reshape(1, -1), a_norm_kv[0].reshape(1, -1), w_uqt, w_uk, w_uvt)

    cos_r, sin_r = _rope_tables(jnp.arange(N_META, N_META + seq))
    cos_m, sin_m = _rope_tables(jnp.arange(N_META))
    tm = _pick_tile(seq, (256, 128, 64, 32, 16))
    (mq, mk, mv, so, sg, gl, glt, qn, qr, kn, va, kr) = _in_proj(x2d, cos_r, sin_r, in_weights, tm)
    (_, mk_m, mvt_m, _, _, _, glt_m, _, _, kn_m, vat_m, kr_m) = _in_proj(
        meta_tokens.astype(x.dtype), cos_m, sin_m, in_weights, N_META)

    n_pad = LANES - N_META
    pad_gate = jnp.concatenate([jnp.full((n_pad, 2 * M_HEADS), NEG, F32),
                                jnp.zeros((n_pad, 2 * M_HEADS), F32)], axis=1)
    front = lambda a: jnp.concatenate([jnp.zeros((n_pad, a.shape[1]), a.dtype), a], axis=0)
    front_t = lambda a: jnp.concatenate([jnp.zeros((a.shape[0], n_pad), a.dtype), a], axis=1)
    mlstm_meta = (front(mk_m), front_t(mvt_m), jnp.concatenate([pad_gate.T, glt_m], axis=1))
    lc = _pick_tile(seq, (256, 128))
    hf, hb = _mlstm(mq, mk, mv, gl, glt, mlstm_meta, batch, lc)

    back = lambda a: jnp.concatenate([a, jnp.zeros((n_pad, a.shape[1]), a.dtype)], axis=0)
    back_t = lambda a: jnp.concatenate([a, jnp.zeros((a.shape[0], n_pad), a.dtype)], axis=1)
    tq = _pick_tile(seq, ATT_Q_TILES)
    tk = _pick_tile(seq, (512, 256, 128))
    att = _attention(qn, qr, kn, kr, va, (back(kn_m), back(kr_m), back_t(vat_m)), batch, tq, tk)

    wr_hi = w_router[0].astype(BF16)
    wr_lo = (w_router[0] - wr_hi.astype(F32)).astype(BF16)
    mix_weights = (m_norm[0].reshape(1, -1), w_m_proj[0].astype(BF16), w_a_proj[0].astype(BF16),
                   w_out[0].astype(BF16), norm_ffn[0].reshape(1, d), wr_hi, wr_lo,
                   b_router[0].reshape(1, -1))
    h1, u2, top_idx, gates = _mix(x2d, hf, hb, so, att, sg, mix_weights,
                                  _pick_tile(rows, (512, 256, 128, 64, 32, 16)))

    tme = 512
    n = rows * TOP_K
    nb = -(-n // tme) + N_EXPERTS
    i32 = jnp.int32
    flat_e = top_idx.reshape(-1)
    pair_ids = jnp.arange(n, dtype=i32)
    _, order = lax.sort((flat_e, pair_ids), num_keys=1, is_stable=True)
    _, rank = lax.sort((order, pair_ids), num_keys=1)
    counts = jnp.sum(flat_e[:, None] == jnp.arange(N_EXPERTS, dtype=i32)[None, :], axis=0, dtype=i32)
    start = jnp.cumsum(counts) - counts
    padded = (counts + tme - 1) // tme * tme
    padded_end = jnp.cumsum(padded)
    padded_start = padded_end - padded
    block_row0 = jnp.arange(nb, dtype=i32) * tme
    block_e = jnp.minimum(jnp.sum(padded_end[None, :] <= block_row0[:, None], axis=1, dtype=i32),
                          N_EXPERTS - 1)
    n_used = (padded_end[-1] // tme).astype(i32).reshape(1)
    off = (block_row0 - padded_start[block_e])[:, None] + jnp.arange(tme, dtype=i32)[None, :]
    valid = off < counts[block_e][:, None]
    src = jnp.clip(start[block_e][:, None] + off, 0, n - 1)
    filler = (block_row0[:, None] + jnp.arange(tme, dtype=i32)[None, :]) % rows
    buf_tok = jnp.where(valid, order[src] // TOP_K, filler).reshape(-1)
    slot = (padded_start - start)[flat_e] + rank
    xg = u2[buf_tok]
    yb = _experts(block_e, n_used, xg, w1[0], b1[0].reshape(N_EXPERTS, 1, -1), w2[0],
                  b2[0].reshape(N_EXPERTS, 1, -1), tme)
    yg = yb[slot.reshape(rows, TOP_K).T.reshape(-1)].reshape(TOP_K, rows, d)

    out = _final(h1, yg, gates, norm_final.reshape(1, d), tm)
    return out.reshape(batch, seq, d)
```

```python
import functools

import jax
import jax.numpy as jnp
import numpy as np
from jax import lax
from jax.experimental import pallas as pl
from jax.experimental.pallas import tpu as pltpu

N_META = 16
EPS = 1e-6
NEG = -1e30
M_HEADS = 8
M_DQK = 64
M_DV = 128
A_HEADS = 8
A_NOPE = 128
A_ROPE = 64
A_DV = 128
Q_LORA = 256
KV_LORA = 128
ROPE_THETA = 10000.0
N_EXPERTS = 32
TOP_K = 4
D_FF = 1024
SWIGLU_ALPHA = 1.702
SWIGLU_LIMIT = 7.0

LANES = 128
ATT_Q_TILES = (1024, 512, 256, 128)
ONES_ROWS = 16
N_GATE_COLS = 4 * M_HEADS
ATT_SCALE = (A_NOPE + A_ROPE) ** -0.5 * 1.4426950408889634
VMEM_LIMIT = 56 * 1024 * 1024

F32 = jnp.float32
BF16 = jnp.bfloat16


def _mm(a, b):
    return jnp.dot(a, b, preferred_element_type=F32)


def _mm_nt(a, b):
    return lax.dot_general(a, b, (((1,), (1,)), ((), ())), preferred_element_type=F32)


def _mm_tn(a, b):
    return lax.dot_general(a, b, (((0,), (0,)), ((), ())), preferred_element_type=F32)


def _mm_exact(a, b):
    return jnp.dot(a, b, preferred_element_type=F32, precision=lax.Precision.HIGHEST)


def _rms(x, g):
    return x * lax.rsqrt(jnp.mean(x * x, axis=-1, keepdims=True) + EPS) * g


def _log_sigmoid(x):
    return jnp.minimum(x, 0.0) - jnp.log1p(jnp.exp(-jnp.abs(x)))


def _sigmoid(x):
    return 1.0 / (1.0 + jnp.exp(-x))


def _resident(shape):
    nd = len(shape)
    return pl.BlockSpec(shape, lambda *_: (0,) * nd, pipeline_mode=pl.Buffered(1))


def _in_proj_kernel(x_ref, cos_ref, sin_ref, cost_ref, sint_ref, gmix_ref, wqvt_ref, wko_ref, wgate_ref,
                    wlat_ref, wif_ref, wift_ref, bif_ref, bift_ref, gq_ref, gkv_ref, wuqt_ref, wuk_ref,
                    wuvt_ref, mqt_ref, mk_ref, mvt_ref, so_ref, sg_ref, gl_ref, glt_ref,
                    qnt_ref, qrt_ref, kn_ref, vat_ref, kr_ref):
    d_qk = M_HEADS * M_DQK
    ub = _rms(x_ref[...], gmix_ref[...]).astype(BF16)

    qvt = _mm_nt(wqvt_ref[...], ub)
    mqt_ref[...] = qvt[:d_qk, :].astype(BF16)
    mvt_ref[...] = qvt[d_qk:, :].astype(BF16)
    ko = _mm(ub, wko_ref[...])
    mk_ref[...] = ko[:, :d_qk].astype(BF16)
    so_ref[...] = _sigmoid(ko[:, d_qk:]).astype(BF16)
    sg_ref[...] = _sigmoid(_mm(ub, wgate_ref[...])).astype(BF16)

    gi = _mm(ub, wif_ref[...]) + bif_ref[...]
    col = lax.broadcasted_iota(jnp.int32, gi.shape, 1)
    gl_ref[...] = jnp.where(col < 2 * M_HEADS, gi, _log_sigmoid(gi))
    git = _mm_nt(wift_ref[...], ub) + bift_ref[...]
    row = lax.broadcasted_iota(jnp.int32, git.shape, 0)
    glt_ref[...] = jnp.where(row < 2 * M_HEADS, git, _log_sigmoid(git))

    lat = _mm(ub, wlat_ref[...])
    cos = cos_ref[...]
    sin = sin_ref[...]
    kr_ref[...] = (lat[:, Q_LORA + KV_LORA:Q_LORA + KV_LORA + LANES] * cos
                   + lat[:, Q_LORA + KV_LORA + LANES:] * sin).astype(BF16)

    cqn = _rms(lat[:, :Q_LORA], gq_ref[...]).astype(BF16)
    q3t = _mm_nt(wuqt_ref[...], cqn)
    d_n = A_HEADS * A_NOPE
    qnt_ref[...] = (q3t[:d_n, :] * ATT_SCALE).astype(BF16)
    cos_t = cost_ref[...]
    sin_t = sint_ref[...]
    for h in range(A_HEADS):
        a = q3t[d_n + h * LANES:d_n + (h + 1) * LANES, :]
        b = q3t[d_n + (A_HEADS + h) * LANES:d_n + (A_HEADS + h + 1) * LANES, :]
        qrt_ref[h * LANES:(h + 1) * LANES, :] = ((a * cos_t + b * sin_t) * ATT_SCALE).astype(BF16)

    ckvn = _rms(lat[:, Q_LORA:Q_LORA + KV_LORA], gkv_ref[...]).astype(BF16)
    kn_ref[...] = _mm(ckvn, wuk_ref[...]).astype(BF16)
    vat_ref[...] = _mm_nt(wuvt_ref[...], ckvn).astype(BF16)


def _in_proj(x2d, cos4, sin4, weights, tm):
    rows, d = x2d.shape
    n_tab = cos4.shape[0] // tm
    row_blk = lambda w: pl.BlockSpec((tm, w), lambda i: (i, 0))
    tab_blk = pl.BlockSpec((tm, LANES), lambda i: (i % n_tab, 0))
    tab_t_blk = pl.BlockSpec((LANES, tm), lambda i: (0, i % n_tab))
    d_qk, d_v, d_n = M_HEADS * M_DQK, M_HEADS * M_DV, A_HEADS * A_NOPE
    out_shape = [
        jax.ShapeDtypeStruct((d_qk, rows), BF16),
        jax.ShapeDtypeStruct((rows, d_qk), BF16),
        jax.ShapeDtypeStruct((d_v, rows), BF16),
        jax.ShapeDtypeStruct((rows, d_v), BF16),
        jax.ShapeDtypeStruct((rows, 2 * d), BF16),
        jax.ShapeDtypeStruct((rows, N_GATE_COLS), F32),
        jax.ShapeDtypeStruct((N_GATE_COLS, rows), F32),
        jax.ShapeDtypeStruct((d_n, rows), BF16),
        jax.ShapeDtypeStruct((A_HEADS * LANES, rows), BF16),
        jax.ShapeDtypeStruct((rows, d_n), BF16),
        jax.ShapeDtypeStruct((A_HEADS * A_DV, rows), BF16),
        jax.ShapeDtypeStruct((rows, LANES), BF16),
    ]
    col_blk = lambda h: pl.BlockSpec((h, tm), lambda i: (0, i))
    out_specs = [col_blk(d_qk), row_blk(d_qk), col_blk(d_v), row_blk(d_v), row_blk(2 * d),
                 row_blk(N_GATE_COLS), col_blk(N_GATE_COLS),
                 col_blk(d_n), col_blk(A_HEADS * LANES), row_blk(d_n), col_blk(A_HEADS * A_DV),
                 row_blk(LANES)]
    in_specs = ([row_blk(d), tab_blk, tab_blk, tab_t_blk, tab_t_blk]
                + [_resident(w.shape) for w in weights])
    return pl.pallas_call(
        _in_proj_kernel,
        grid=(rows // tm,),
        in_specs=in_specs,
        out_specs=out_specs,
        out_shape=out_shape,
        compiler_params=pltpu.CompilerParams(dimension_semantics=("parallel",),
                                             vmem_limit_bytes=VMEM_LIMIT),
        name="in_proj",
    )(x2d, cos4, sin4, cos4.T, sin4.T, *weights)


def _mlstm_kernel(qf_ref, kf_ref, vf_ref, gf_ref, gtf_ref,
                  qb_ref, kb_ref, vb_ref, gb_ref, gtb_ref,
                  km_ref, vm_ref, gtm_ref,
                  hf_ref, hb_ref, c_scr, m_scr):
    c_idx = pl.program_id(1)
    lc = kf_ref.shape[0]
    lm = km_ref.shape[0]
    n_pairs = M_HEADS // 2
    half = LANES // 2
    log2e = 1.4426950408889634

    def tri(n, lower):
        r = lax.broadcasted_iota(jnp.int32, (n, n), 0)
        c = lax.broadcasted_iota(jnp.int32, (n, n), 1)
        return (r >= c) if lower else (r <= c)

    def as_bf16(mask):
        return jnp.where(mask, 1.0, 0.0).astype(BF16)

    def eye(n):
        r = lax.broadcasted_iota(jnp.int32, (n, n), 0)
        c = lax.broadcasted_iota(jnp.int32, (n, n), 1)
        return as_bf16(r == c)

    eye_t = eye(lc)
    lane = lax.broadcasted_iota(jnp.int32, (1, LANES), 1)
    sub128 = lax.broadcasted_iota(jnp.int32, (LANES, 1), 0)

    def lane_mask(h):
        return (lane < half) if h % 2 == 0 else (lane >= half)

    def sublane_mask(h):
        return (sub128 < half) if h % 2 == 0 else (sub128 >= half)

    def split3(x):
        hi = x.astype(BF16)
        r1 = x - hi.astype(F32)
        mid = r1.astype(BF16)
        lo = (r1 - mid.astype(F32)).astype(BF16)
        return hi, mid, lo

    def cum_rows(g_rows, tri_b):
        hi, mid, lo = split3(g_rows)
        return _mm(hi, tri_b) + _mm(mid, tri_b) + _mm(lo, tri_b)

    def cum_cols(tri_b, g_cols):
        hi, mid, lo = split3(g_cols)
        return _mm(tri_b, hi) + _mm(tri_b, mid) + _mm(tri_b, lo)

    def state_update(direction, pair, k2, vt_pair, cum_row, g_rows, tot_col):
        n = k2.shape[0]
        d_c = None
        decays = []
        for sub in range(2):
            h = 2 * pair + sub
            j = direction * M_HEADS + h
            jf = 2 * M_HEADS + j
            m_prev = m_scr[j][0:1, 0:1]
            tot = tot_col[jf:jf + 1, :]
            g_row = tot - cum_row[jf:jf + 1, :] + g_rows[j:j + 1, :]
            m_new = jnp.maximum(tot + m_prev, jnp.max(g_row, axis=1, keepdims=True))
            decays.append(jnp.exp2(tot + m_prev - m_new))
            wk = jnp.exp2(g_row - m_new)
            vw = jnp.concatenate([(vt_pair[sub].astype(F32) * wk).astype(BF16),
                                  jnp.broadcast_to(wk.astype(BF16), (ONES_ROWS, n))], axis=0)
            km = jnp.where(lane_mask(h), k2, jnp.zeros_like(k2))
            upd = _mm(vw, km)
            d_c = upd if d_c is None else d_c + upd
            m_scr[j] = jnp.broadcast_to(m_new, m_scr.shape[1:])
        decay_cols = jnp.where(lane < half, decays[0], decays[1])
        ci = direction * n_pairs + pair
        c_scr[ci] = decay_cols * c_scr[ci] + d_c

    @pl.when(c_idx == 0)
    def _init():
        c_scr[...] = jnp.zeros_like(c_scr)
        m_scr[...] = jnp.zeros_like(m_scr)
        g_rows = gtm_ref[...] * log2e
        cum_m = cum_rows(g_rows, as_bf16(tri(lm, False)))
        tot_m = cum_m[:, lm - 1:lm]
        for pair in range(n_pairs):
            k2 = km_ref[:, pair * LANES:(pair + 1) * LANES]
            vt_pair = [vm_ref[(2 * pair + s) * M_DV:(2 * pair + s + 1) * M_DV, :] for s in range(2)]
            state_update(0, pair, k2, vt_pair, cum_m, g_rows, tot_m)

    views = ((qf_ref, kf_ref, vf_ref, gf_ref, gtf_ref, hf_ref),
             (qb_ref, kb_ref, vb_ref, gb_ref, gtb_ref, hb_ref))
    shared = []
    for direction, (_, _, _, g_ref, gt_ref, _) in enumerate(views):
        fwd = direction == 0
        g_cols = g_ref[...] * log2e
        g_rows = gt_ref[...] * log2e
        cum_row = cum_rows(g_rows, as_bf16(tri(lc, not fwd)))
        cum_col = cum_cols(as_bf16(tri(lc, fwd)), g_cols)
        tot_col = cum_row[:, lc - 1:lc] if fwd else cum_row[:, 0:1]
        key_cols = g_cols[:, :2 * M_HEADS] - cum_col[:, 2 * M_HEADS:]
        valid = tri(lc, not fwd)
        shared.append((g_rows, cum_row, tot_col, key_cols, valid))

    units = [(d, p, s) for d in range(2) for p in range(n_pairs) for s in range(2)]
    pair_vals = {}
    vals = {}

    def scores(u):
        direction, pair, sub = units[u]
        q_ref, k_ref, v_ref = views[direction][:3]
        h = 2 * pair + sub
        if sub == 0:
            q2t = q_ref[pair * LANES:(pair + 1) * LANES, :]
            k2 = k_ref[:, pair * LANES:(pair + 1) * LANES]
            c_b = c_scr[direction * n_pairs + pair].astype(BF16)
            pair_vals[(direction, pair)] = (q2t, k2, c_b, [])
        q2t, k2, _, vt_pair = pair_vals[(direction, pair)]
        vt = v_ref[h * M_DV:(h + 1) * M_DV, :]
        vt_pair.append(vt)
        qmt = jnp.where(sublane_mask(h), q2t, jnp.zeros_like(q2t))
        vals[u] = dict(vt=vt, qmt=qmt, st=_mm(k2, qmt))

    def accumulate(u):
        direction, pair, sub = units[u]
        g_rows, cum_row, tot_col, key_cols, valid = shared[direction]
        _, k2, c_b, vt_pair = pair_vals[(direction, pair)]
        j = direction * M_HEADS + 2 * pair + sub
        jf = 2 * M_HEADS + j
        v = vals[u]
        b_row = cum_row[jf:jf + 1, :]
        dt = jnp.where(valid, key_cols[:, j:j + 1] + b_row, NEG)
        m_inter = b_row + m_scr[j][0:1, 0:1]
        m_t = jnp.maximum(m_inter, jnp.max(dt, axis=0, keepdims=True))
        wt = (jnp.exp2(dt - m_t) * v['st']).astype(BF16)
        qs = (v['qmt'].astype(F32) * jnp.exp2(m_inter - m_t)).astype(BF16)
        vext = jnp.concatenate([v['vt'], jnp.ones((ONES_ROWS, lc), BF16)], axis=0)
        v['acc'] = _mm(vext, wt) + _mm(c_b, qs)
        v['m_t'] = m_t
        if sub == 1:
            state_update(direction, pair, k2, vt_pair, cum_row, g_rows, tot_col)

    def finish(u):
        direction, pair, sub = units[u]
        h_ref = views[direction][5]
        h = 2 * pair + sub
        v = vals.pop(u)
        den = jnp.maximum(jnp.abs(v['acc'][M_DV:M_DV + 1, :]), jnp.exp2(-v['m_t']))
        ht = (v['acc'][:M_DV, :] * (1.0 / den)).astype(BF16)
        h_ref[:, h * M_DV:(h + 1) * M_DV] = _mm_nt(eye_t, ht).astype(h_ref.dtype)

    for t in range(len(units) + 2):
        if t < len(units):
            scores(t)
        if 0 <= t - 1 < len(units):
            accumulate(t - 1)
        if 0 <= t - 2 < len(units):
            finish(t - 2)


def _mlstm(mq, mk, mv, gl, glt, meta, batch, lc):
    rows = mk.shape[0]
    seq = rows // batch
    nc = seq // lc
    km, vm, gtm = meta
    d_qk, d_v = M_HEADS * M_DQK, M_HEADS * M_DV

    fwd_row = lambda b, c: (b * nc + c, 0)
    bwd_row = lambda b, c: (b * nc + nc - 1 - c, 0)
    fwd_col = lambda b, c: (0, b * nc + c)
    bwd_col = lambda b, c: (0, b * nc + nc - 1 - c)

    def views(row_map, col_map):
        return [pl.BlockSpec((d_qk, lc), col_map), pl.BlockSpec((lc, d_qk), row_map),
                pl.BlockSpec((d_v, lc), col_map), pl.BlockSpec((lc, N_GATE_COLS), row_map),
                pl.BlockSpec((N_GATE_COLS, lc), col_map)]

    const = lambda a: pl.BlockSpec(a.shape, lambda b, c: (0, 0))
    in_specs = views(fwd_row, fwd_col) + views(bwd_row, bwd_col) + [const(a) for a in meta]
    return pl.pallas_call(
        _mlstm_kernel,
        grid=(batch, nc),
        in_specs=in_specs,
        out_specs=[pl.BlockSpec((lc, d_v), fwd_row), pl.BlockSpec((lc, d_v), bwd_row)],
        out_shape=[jax.ShapeDtypeStruct((rows, d_v), BF16)] * 2,
        scratch_shapes=[pltpu.VMEM((2 * (M_HEADS // 2), M_DV + ONES_ROWS, LANES), F32),
                        pltpu.VMEM((2 * M_HEADS, 8, LANES), F32)],
        compiler_params=pltpu.CompilerParams(dimension_semantics=("parallel", "arbitrary"),
                                             vmem_limit_bytes=VMEM_LIMIT),
        name="mlstm",
    )(mq, mk, mv, gl, glt, mq, mk, mv, gl, glt, km, vm, gtm)


def _attn_kernel(qnt_ref, qrt_ref, kn_ref, kr_ref, vt_ref, knm_ref, krm_ref, vtm_ref, o_ref, *, tk):
    qt = jnp.concatenate([qnt_ref[...], qrt_ref[...]], axis=0)

    def vext(vt):
        return jnp.concatenate([vt, jnp.ones((ONES_ROWS, vt.shape[1]), BF16)], axis=0)

    km = jnp.concatenate([knm_ref[...], krm_ref[...]], axis=1)
    s_meta = _mm(km, qt)
    key = lax.broadcasted_iota(jnp.int32, s_meta.shape, 0)
    s_meta = jnp.where(key < N_META, s_meta, NEG)
    n_chunks = kn_ref.shape[0] // tk

    def scores(j):
        rows = slice(j * tk, (j + 1) * tk)
        k = jnp.concatenate([kn_ref[rows, :], kr_ref[rows, :]], axis=1)
        return _mm(k, qt)

    s_next = scores(0)
    m = jnp.max(s_meta, axis=0, keepdims=True)
    acc = None
    for j in range(n_chunks):
        s = s_next
        if j + 1 < n_chunks:
            s_next = scores(j + 1)
        m_new = jnp.maximum(m, jnp.max(s, axis=0, keepdims=True))
        p = jnp.exp2(s - m_new).astype(BF16)
        v = vext(vt_ref[:, j * tk:(j + 1) * tk])
        if j == 0:
            p = jnp.concatenate([jnp.exp2(s_meta - m_new).astype(BF16), p], axis=0)
            acc = _mm(jnp.concatenate([vext(vtm_ref[...]), v], axis=1), p)
        else:
            acc = jnp.exp2(m - m_new) * acc + _mm(v, p)
        m = m_new
    out_t = acc[:A_DV, :] * (1.0 / acc[A_DV:A_DV + 1, :])
    o_ref[...] = out_t.T.astype(o_ref.dtype)


def _attention(qnt, qrt, kn, kr, vat, meta, batch, tq, tk):
    rows = kn.shape[0]
    seq = rows // batch
    nq = seq // tq
    knm, krm, vatm = meta
    n_meta_pad = knm.shape[0]
    qt_blk = pl.BlockSpec((LANES, tq), lambda b, h, qi: (h, b * nq + qi))
    k_blk = pl.BlockSpec((seq, LANES), lambda b, h, qi: (b, h))
    kr_blk = pl.BlockSpec((seq, LANES), lambda b, h, qi: (b, 0))
    vt_blk = pl.BlockSpec((A_DV, seq), lambda b, h, qi: (h, b))
    meta_h = pl.BlockSpec((n_meta_pad, LANES), lambda b, h, qi: (0, h))
    meta_0 = pl.BlockSpec((n_meta_pad, LANES), lambda b, h, qi: (0, 0))
    meta_vt = pl.BlockSpec((A_DV, n_meta_pad), lambda b, h, qi: (h, 0))
    return pl.pallas_call(
        functools.partial(_attn_kernel, tk=tk),
        grid=(batch, A_HEADS, nq),
        in_specs=[qt_blk, qt_blk, k_blk, kr_blk, vt_blk, meta_h, meta_0, meta_vt],
        out_specs=pl.BlockSpec((tq, A_DV), lambda b, h, qi: (b * nq + qi, h)),
        out_shape=jax.ShapeDtypeStruct((rows, A_HEADS * A_DV), BF16),
        compiler_params=pltpu.CompilerParams(
            dimension_semantics=("parallel", "parallel", "parallel"),
            vmem_limit_bytes=VMEM_LIMIT),
        name="attention",
    )(qnt, qrt, kn, kr, vat, knm, krm, vatm)


def _mix_kernel(x_ref, hf_ref, hb_ref, so_ref, att_ref, sg_ref, gm_ref, wm_ref, wa_ref, wo_ref,
                gffn_ref, wrh_ref, wrl_ref, br_ref, h1_ref, u2_ref, idx_ref, gate_ref, *, n_split):
    d = x_ref.shape[1]
    tr = x_ref.shape[0] // n_split
    for part in range(n_split):
        rows = slice(part * tr, (part + 1) * tr)
        hs = hf_ref[rows, :].astype(F32) + hb_ref[rows, :].astype(F32)
        parts = []
        for h in range(M_HEADS):
            blk = hs[:, h * M_DV:(h + 1) * M_DV]
            parts.append(blk * lax.rsqrt(jnp.mean(blk * blk, axis=-1, keepdims=True) + EPS))
        hn = jnp.concatenate(parts, axis=1) * gm_ref[...] * so_ref[rows, :].astype(F32)
        y_m = _mm(hn.astype(BF16), wm_ref[...])
        y_a = _mm(att_ref[rows, :], wa_ref[...])
        sg = sg_ref[rows, :].astype(F32)
        mixed = sg[:, :d] * y_m + sg[:, d:] * y_a
        h1 = x_ref[rows, :] + _mm(mixed.astype(BF16), wo_ref[...])
        h1_ref[rows, :] = h1
        u2 = _rms(h1, gffn_ref[...])
        u2_hi = u2.astype(BF16)
        u2_ref[rows, :] = u2_hi
        u2_lo = (u2 - u2_hi.astype(F32)).astype(BF16)
        logits = (_mm(u2_hi, wrh_ref[...]) + _mm(u2_hi, wrl_ref[...]) + _mm(u2_lo, wrh_ref[...])
                  + br_ref[...])
        lane = lax.broadcasted_iota(jnp.int32, logits.shape, 1).astype(F32)
        out_lane = lax.broadcasted_iota(jnp.int32, (tr, TOP_K), 1)
        work = logits
        top_idx = jnp.zeros((tr, TOP_K), F32)
        top_exp = jnp.zeros((tr, TOP_K), F32)
        v_max = None
        for k in range(TOP_K):
            v = jnp.max(work, axis=-1, keepdims=True)
            idx = jnp.min(jnp.where(work == v, lane, float(N_EXPERTS)), axis=-1, keepdims=True)
            work = jnp.where(lane == idx, -jnp.inf, work)
            v_max = v if k == 0 else v_max
            top_idx = jnp.where(out_lane == k, idx, top_idx)
            top_exp = jnp.where(out_lane == k, jnp.exp(v - v_max), top_exp)
        idx_ref[rows, :] = top_idx.astype(jnp.int32)
        gate_ref[rows, :] = top_exp / jnp.sum(top_exp, axis=-1, keepdims=True)


def _mix(x2d, hf, hb, so, att, sg, weights, tm):
    rows, d = x2d.shape
    row_blk = lambda w: pl.BlockSpec((tm, w), lambda i: (i, 0))
    in_specs = ([row_blk(d), row_blk(d), row_blk(d), row_blk(d), row_blk(d), row_blk(2 * d)]
                + [_resident(w.shape) for w in weights])
    return pl.pallas_call(
        functools.partial(_mix_kernel, n_split=2 if tm % 32 == 0 else 1),
        grid=(rows // tm,),
        in_specs=in_specs,
        out_specs=[row_blk(d), row_blk(d), row_blk(TOP_K), row_blk(TOP_K)],
        out_shape=[jax.ShapeDtypeStruct((rows, d), F32), jax.ShapeDtypeStruct((rows, d), BF16),
                   jax.ShapeDtypeStruct((rows, TOP_K), jnp.int32),
                   jax.ShapeDtypeStruct((rows, TOP_K), F32)],
        compiler_params=pltpu.CompilerParams(dimension_semantics=("parallel",),
                                             vmem_limit_bytes=VMEM_LIMIT),
        name="mix",
    )(x2d, hf, hb, so, att, sg, *weights)


def _expert_kernel(be_ref, first_ref, next_ref, slot_ref, nused_ref,
                   x_ref, w1_hbm, b1_ref, w2_hbm, b2_ref, y_ref,
                   w1f_scr, w2f_scr, w1b_scr, w2b_scr, sem):
    i = pl.program_id(0)

    def weight_copies(e, slot):
        return (pltpu.make_async_copy(w1_hbm.at[e], w1f_scr.at[slot], sem.at[0, slot]),
                pltpu.make_async_copy(w2_hbm.at[e], w2f_scr.at[slot], sem.at[1, slot]))

    @pl.when(i < nused_ref[0])
    def _():
        @pl.when(first_ref[i] == 1)
        def _switch():
            slot = slot_ref[i]

            @pl.when(i == 0)
            def _():
                for c in weight_copies(be_ref[i], slot):
                    c.start()

            @pl.when(next_ref[i] >= 0)
            def _():
                for c in weight_copies(next_ref[i], 1 - slot):
                    c.start()

            for c in weight_copies(be_ref[i], slot):
                c.wait()
            w1b_scr[...] = w1f_scr[slot].astype(BF16)
            w2b_scr[...] = w2f_scr[slot].astype(BF16)

        hid = _mm(x_ref[...], w1b_scr[...]) + b1_ref[...]
        glu = jnp.minimum(hid[:, :D_FF], SWIGLU_LIMIT)
        lin = jnp.clip(hid[:, D_FF:], -SWIGLU_LIMIT, SWIGLU_LIMIT)
        act = glu * _sigmoid(SWIGLU_ALPHA * glu) * (lin + 1.0)
        y_ref[...] = (_mm(act.astype(BF16), w2b_scr[...]) + b2_ref[...]).astype(y_ref.dtype)

    @pl.when(i >= nused_ref[0])
    def _():
        y_ref[...] = jnp.zeros_like(y_ref)


def _experts(block_e, n_used, xg, w1, b1, w2, b2, tme):
    rows, d = xg.shape
    nb = rows // tme
    i32 = jnp.int32
    used = jnp.arange(nb, dtype=i32) < n_used[0]
    first = jnp.concatenate([jnp.ones((1,), i32), (block_e[1:] != block_e[:-1]).astype(i32)])
    later = (block_e[None, :] > block_e[:, None]) & used[None, :]
    nxt = jnp.min(jnp.where(later, block_e[None, :], N_EXPERTS), axis=1)
    nxt = jnp.where(nxt == N_EXPERTS, -1, nxt).astype(i32)
    slot = ((jnp.cumsum(first) - 1) % 2).astype(i32)
    row_map = lambda i, *_: (i, 0)
    bias_map = lambda i, be, *_: (be[i], 0, 0)
    grid_spec = pltpu.PrefetchScalarGridSpec(
        num_scalar_prefetch=5,
        grid=(nb,),
        in_specs=[
            pl.BlockSpec((tme, d), row_map),
            pl.BlockSpec(memory_space=pl.ANY),
            pl.BlockSpec((None, 1, 2 * D_FF), bias_map),
            pl.BlockSpec(memory_space=pl.ANY),
            pl.BlockSpec((None, 1, d), bias_map),
        ],
        out_specs=pl.BlockSpec((tme, d), row_map),
        scratch_shapes=[pltpu.VMEM((2, d, 2 * D_FF), F32), pltpu.VMEM((2, D_FF, d), F32),
                        pltpu.VMEM((d, 2 * D_FF), BF16), pltpu.VMEM((D_FF, d), BF16),
                        pltpu.SemaphoreType.DMA((2, 2))],
    )
    return pl.pallas_call(
        _expert_kernel,
        grid_spec=grid_spec,
        out_shape=jax.ShapeDtypeStruct((rows, d), BF16),
        compiler_params=pltpu.CompilerParams(dimension_semantics=("arbitrary",),
                                             vmem_limit_bytes=VMEM_LIMIT),
        name="experts",
    )(block_e, first, nxt, slot, n_used, xg, w1, b1, w2, b2)


def _final_kernel(h1_ref, yg_ref, gate_ref, g_ref, o_ref):
    gates = gate_ref[...]
    y = h1_ref[...]
    for k in range(TOP_K):
        y = y + yg_ref[k].astype(F32) * gates[:, k:k + 1]
    o_ref[...] = _rms(y, g_ref[...])


def _final(h1, yg, gates, g, tm):
    rows, d = h1.shape
    row_blk = pl.BlockSpec((tm, d), lambda i: (i, 0))
    return pl.pallas_call(
        _final_kernel,
        grid=(rows // tm,),
        in_specs=[row_blk, pl.BlockSpec((TOP_K, tm, d), lambda i: (0, i, 0)),
                  pl.BlockSpec((tm, TOP_K), lambda i: (i, 0)), pl.BlockSpec((1, d), lambda i: (0, 0))],
        out_specs=row_blk,
        out_shape=jax.ShapeDtypeStruct((rows, d), F32),
        compiler_params=pltpu.CompilerParams(dimension_semantics=("parallel",)),
        name="final",
    )(h1, yg, gates, g)


def _rope_tables(pos):
    half = A_ROPE // 2
    freqs = ROPE_THETA ** (-jnp.arange(half, dtype=F32) / half)
    ang = pos.astype(F32)[:, None] * freqs[None, :]
    zeros = jnp.zeros((pos.shape[0], LANES - A_ROPE), F32)
    cos, sin = jnp.cos(ang), jnp.sin(ang)
    return jnp.concatenate([cos, cos, zeros], axis=1), jnp.concatenate([sin, sin, zeros], axis=1)


def _rope_pair(w):
    half = A_ROPE // 2
    zeros = jnp.zeros(w.shape[:-1] + (LANES - A_ROPE,), w.dtype)
    a = jnp.concatenate([w, zeros], axis=-1)
    b = jnp.concatenate([-w[..., half:], w[..., :half], zeros], axis=-1)
    return a, b


def _pick_tile(n, prefs):
    for t in prefs:
        if n % t == 0:
            return t
    return n


def kernel(x, meta_tokens, norm_mix, w_in, m_b_i, m_b_f, m_norm, w_m_proj, a_w_uq, a_w_ukv,
           a_norm_q, a_norm_kv, w_a_proj, w_out, norm_ffn, w_router, b_router, w1, b1, w2, b2,
           norm_final):
    assert norm_mix.shape[0] == 1, "single-layer encoder"
    batch, seq, d = x.shape
    rows = batch * seq
    x2d = x.reshape(rows, d)

    sizes = (M_HEADS * M_DQK, M_HEADS * M_DQK, M_HEADS * M_DV, M_HEADS * M_DV, 2 * M_HEADS,
             2 * M_HEADS, Q_LORA, KV_LORA, A_ROPE, 2 * d)
    offs = np.cumsum((0,) + sizes)
    w_q, w_k, w_v, w_o, w_i, w_f, w_cq, w_ckv, w_kr, w_g = (
        w_in[0][:, offs[i]:offs[i + 1]] for i in range(len(sizes)))
    kr_a, kr_b = _rope_pair(w_kr)
    w_qvt = jnp.concatenate([w_q * (M_DQK ** -0.5), w_v], axis=1).T.astype(BF16)
    w_ko = jnp.concatenate([w_k, w_o], axis=1).astype(BF16)
    w_lat = jnp.concatenate([w_cq, w_ckv, kr_a, kr_b], axis=1).astype(BF16)
    w_if = jnp.concatenate([w_i, w_f], axis=1).astype(BF16)
    b_if = jnp.concatenate([m_b_i[0].reshape(-1), m_b_f[0].reshape(-1)]).astype(F32)
    uq_a, uq_b = _rope_pair(a_w_uq[0][:, :, A_NOPE:])
    w_uqt = jnp.concatenate([a_w_uq[0][:, :, :A_NOPE].reshape(Q_LORA, -1), uq_a.reshape(Q_LORA, -1),
                             uq_b.reshape(Q_LORA, -1)], axis=1).T.astype(BF16)
    w_uk = a_w_ukv[0][:, :, :A_NOPE].reshape(KV_LORA, -1).astype(BF16)
    w_uvt = a_w_ukv[0][:, :, A_NOPE:].reshape(KV_LORA, -1).T.astype(BF16)
    in_weights = (norm_mix[0].reshape(1, d), w_qvt, w_ko, w_g.astype(BF16), w_lat,
                  w_if, w_if.T, b_if.reshape(1, -1), b_if.reshape(-1, 1),
                  a_norm_q[0].reshape(1, -1), a_norm_kv[0].reshape(1, -1), w_uqt, w_uk, w_uvt)

    cos_r, sin_r = _rope_tables(jnp.arange(N_META, N_META + seq))
    cos_m, sin_m = _rope_tables(jnp.arange(N_META))
    tm = _pick_tile(seq, (256, 128, 64, 32, 16))
    (mq, mk, mv, so, sg, gl, glt, qn, qr, kn, va, kr) = _in_proj(
        x2d, cos_r, sin_r, in_weights, _pick_tile(seq, (512, 256, 128, 64, 32, 16)))
    (_, mk_m, mvt_m, _, _, _, glt_m, _, _, kn_m, vat_m, kr_m) = _in_proj(
        meta_tokens.astype(x.dtype), cos_m, sin_m, in_weights, N_META)

    n_pad = LANES - N_META
    pad_gate = jnp.concatenate([jnp.full((n_pad, 2 * M_HEADS), NEG, F32),
                                jnp.zeros((n_pad, 2 * M_HEADS), F32)], axis=1)
    front = lambda a: jnp.concatenate([jnp.zeros((n_pad, a.shape[1]), a.dtype), a], axis=0)
    front_t = lambda a: jnp.concatenate([jnp.zeros((a.shape[0], n_pad), a.dtype), a], axis=1)
    mlstm_meta = (front(mk_m), front_t(mvt_m), jnp.concatenate([pad_gate.T, glt_m], axis=1))
    lc = _pick_tile(seq, (256, 128))
    hf, hb = _mlstm(mq, mk, mv, gl, glt, mlstm_meta, batch, lc)

    back = lambda a: jnp.concatenate([a, jnp.zeros((n_pad, a.shape[1]), a.dtype)], axis=0)
    back_t = lambda a: jnp.concatenate([a, jnp.zeros((a.shape[0], n_pad), a.dtype)], axis=1)
    tq = _pick_tile(seq, ATT_Q_TILES)
    tk = _pick_tile(seq, (512, 256, 128))
    att = _attention(qn, qr, kn, kr, va, (back(kn_m), back(kr_m), back_t(vat_m)), batch, tq, tk)

    wr_hi = w_router[0].astype(BF16)
    wr_lo = (w_router[0] - wr_hi.astype(F32)).astype(BF16)
    mix_weights = (m_norm[0].reshape(1, -1), w_m_proj[0].astype(BF16), w_a_proj[0].astype(BF16),
                   w_out[0].astype(BF16), norm_ffn[0].reshape(1, d), wr_hi, wr_lo,
                   b_router[0].reshape(1, -1))
    h1, u2, top_idx, gates = _mix(x2d, hf, hb, so, att, sg, mix_weights,
                                  _pick_tile(rows, (512, 256, 128, 64, 32, 16)))

    tme = 512
    n = rows * TOP_K
    nb = -(-n // tme) + N_EXPERTS
    i32 = jnp.int32
    flat_e = top_idx.reshape(-1)
    pair_ids = jnp.arange(n, dtype=i32)
    _, order = lax.sort((flat_e, pair_ids), num_keys=1, is_stable=True)
    _, rank = lax.sort((order, pair_ids), num_keys=1)
    counts = jnp.sum(flat_e[:, None] == jnp.arange(N_EXPERTS, dtype=i32)[None, :], axis=0, dtype=i32)
    start = jnp.cumsum(counts) - counts
    padded = (counts + tme - 1) // tme * tme
    padded_end = jnp.cumsum(padded)
    padded_start = padded_end - padded
    block_row0 = jnp.arange(nb, dtype=i32) * tme
    block_e = jnp.minimum(jnp.sum(padded_end[None, :] <= block_row0[:, None], axis=1, dtype=i32),
                          N_EXPERTS - 1)
    n_used = (padded_end[-1] // tme).astype(i32).reshape(1)
    off = (block_row0 - padded_start[block_e])[:, None] + jnp.arange(tme, dtype=i32)[None, :]
    valid = off < counts[block_e][:, None]
    src = jnp.clip(start[block_e][:, None] + off, 0, n - 1)
    filler = (block_row0[:, None] + jnp.arange(tme, dtype=i32)[None, :]) % rows
    buf_tok = jnp.where(valid, order[src] // TOP_K, filler).reshape(-1)
    slot = (padded_start - start)[flat_e] + rank
    xg = u2[buf_tok]
    yb = _experts(block_e, n_used, xg, w1[0], b1[0].reshape(N_EXPERTS, 1, -1), w2[0],
                  b2[0].reshape(N_EXPERTS, 1, -1), tme)
    yg = yb[slot.reshape(rows, TOP_K).T.reshape(-1)].reshape(TOP_K, rows, d)

    out = _final(h1, yg, gates, norm_final.reshape(1, d), tm)
    return out.reshape(batch, seq, d)
```

```python
import functools

import jax
import jax.numpy as jnp
import numpy as np
from jax import lax
from jax.experimental import pallas as pl
from jax.experimental.pallas import tpu as pltpu

N_META = 16
EPS = 1e-6
NEG = -1e30
M_HEADS = 8
M_DQK = 64
M_DV = 128
A_HEADS = 8
A_NOPE = 128
A_ROPE = 64
A_DV = 128
Q_LORA = 256
KV_LORA = 128
ROPE_THETA = 10000.0
N_EXPERTS = 32
TOP_K = 4
D_FF = 1024
SWIGLU_ALPHA = 1.702
SWIGLU_LIMIT = 7.0

LANES = 128
ATT_Q_TILES = (1024, 512, 256, 128)
STAGE_SKEW = 2
ONES_ROWS = 16
N_GATE_COLS = 4 * M_HEADS
ATT_SCALE = (A_NOPE + A_ROPE) ** -0.5 * 1.4426950408889634
VMEM_LIMIT = 56 * 1024 * 1024

F32 = jnp.float32
BF16 = jnp.bfloat16


def _mm(a, b):
    return jnp.dot(a, b, preferred_element_type=F32)


def _mm_nt(a, b):
    return lax.dot_general(a, b, (((1,), (1,)), ((), ())), preferred_element_type=F32)


def _mm_tn(a, b):
    return lax.dot_general(a, b, (((0,), (0,)), ((), ())), preferred_element_type=F32)


def _mm_exact(a, b):
    return jnp.dot(a, b, preferred_element_type=F32, precision=lax.Precision.HIGHEST)


def _rms(x, g):
    return x * lax.rsqrt(jnp.mean(x * x, axis=-1, keepdims=True) + EPS) * g


def _log_sigmoid(x):
    return jnp.minimum(x, 0.0) - jnp.log1p(jnp.exp(-jnp.abs(x)))


def _sigmoid(x):
    return 1.0 / (1.0 + jnp.exp(-x))


def _resident(shape):
    nd = len(shape)
    return pl.BlockSpec(shape, lambda *_: (0,) * nd, pipeline_mode=pl.Buffered(1))


def _in_proj_kernel(x_ref, cos_ref, sin_ref, cost_ref, sint_ref, gmix_ref, wqvt_ref, wko_ref, wgate_ref,
                    wlat_ref, wif_ref, wift_ref, bif_ref, bift_ref, gq_ref, gkv_ref, wuqt_ref, wuk_ref,
                    wuvt_ref, mqt_ref, mk_ref, mvt_ref, so_ref, sg_ref, gl_ref, glt_ref,
                    qnt_ref, qrt_ref, kn_ref, vat_ref, kr_ref, *, n_split):
    d_qk = M_HEADS * M_DQK
    d_n = A_HEADS * A_NOPE
    tr = x_ref.shape[0] // n_split
    for part in range(n_split):
        rows = slice(part * tr, (part + 1) * tr)
        ub = _rms(x_ref[rows, :], gmix_ref[...]).astype(BF16)

        qvt = _mm_nt(wqvt_ref[...], ub)
        mqt_ref[:, rows] = qvt[:d_qk, :].astype(BF16)
        mvt_ref[:, rows] = qvt[d_qk:, :].astype(BF16)
        ko = _mm(ub, wko_ref[...])
        mk_ref[rows, :] = ko[:, :d_qk].astype(BF16)
        so_ref[rows, :] = _sigmoid(ko[:, d_qk:]).astype(BF16)
        sg_ref[rows, :] = _sigmoid(_mm(ub, wgate_ref[...])).astype(BF16)

        gi = _mm(ub, wif_ref[...]) + bif_ref[...]
        col = lax.broadcasted_iota(jnp.int32, gi.shape, 1)
        gl_ref[rows, :] = jnp.where(col < 2 * M_HEADS, gi, _log_sigmoid(gi))
        git = _mm_nt(wift_ref[...], ub) + bift_ref[...]
        row = lax.broadcasted_iota(jnp.int32, git.shape, 0)
        glt_ref[:, rows] = jnp.where(row < 2 * M_HEADS, git, _log_sigmoid(git))

        lat = _mm(ub, wlat_ref[...])
        kr_ref[rows, :] = (lat[:, Q_LORA + KV_LORA:Q_LORA + KV_LORA + LANES] * cos_ref[rows, :]
                           + lat[:, Q_LORA + KV_LORA + LANES:] * sin_ref[rows, :]).astype(BF16)

        cqn = _rms(lat[:, :Q_LORA], gq_ref[...]).astype(BF16)
        q3t = _mm_nt(wuqt_ref[...], cqn)
        qnt_ref[:, rows] = (q3t[:d_n, :] * ATT_SCALE).astype(BF16)
        cos_t = cost_ref[:, rows]
        sin_t = sint_ref[:, rows]
        for h in range(A_HEADS):
            a = q3t[d_n + h * LANES:d_n + (h + 1) * LANES, :]
            b = q3t[d_n + (A_HEADS + h) * LANES:d_n + (A_HEADS + h + 1) * LANES, :]
            qrt_ref[h * LANES:(h + 1) * LANES, rows] = ((a * cos_t + b * sin_t) * ATT_SCALE).astype(BF16)

        ckvn = _rms(lat[:, Q_LORA:Q_LORA + KV_LORA], gkv_ref[...]).astype(BF16)
        kn_ref[rows, :] = _mm(ckvn, wuk_ref[...]).astype(BF16)
        vat_ref[:, rows] = _mm_nt(wuvt_ref[...], ckvn).astype(BF16)


def _in_proj(x2d, cos4, sin4, weights, tm):
    rows, d = x2d.shape
    n_tab = cos4.shape[0] // tm
    row_blk = lambda w: pl.BlockSpec((tm, w), lambda i: (i, 0))
    tab_blk = pl.BlockSpec((tm, LANES), lambda i: (i % n_tab, 0))
    tab_t_blk = pl.BlockSpec((LANES, tm), lambda i: (0, i % n_tab))
    d_qk, d_v, d_n = M_HEADS * M_DQK, M_HEADS * M_DV, A_HEADS * A_NOPE
    out_shape = [
        jax.ShapeDtypeStruct((d_qk, rows), BF16),
        jax.ShapeDtypeStruct((rows, d_qk), BF16),
        jax.ShapeDtypeStruct((d_v, rows), BF16),
        jax.ShapeDtypeStruct((rows, d_v), BF16),
        jax.ShapeDtypeStruct((rows, 2 * d), BF16),
        jax.ShapeDtypeStruct((rows, N_GATE_COLS), F32),
        jax.ShapeDtypeStruct((N_GATE_COLS, rows), F32),
        jax.ShapeDtypeStruct((d_n, rows), BF16),
        jax.ShapeDtypeStruct((A_HEADS * LANES, rows), BF16),
        jax.ShapeDtypeStruct((rows, d_n), BF16),
        jax.ShapeDtypeStruct((A_HEADS * A_DV, rows), BF16),
        jax.ShapeDtypeStruct((rows, LANES), BF16),
    ]
    col_blk = lambda h: pl.BlockSpec((h, tm), lambda i: (0, i))
    out_specs = [col_blk(d_qk), row_blk(d_qk), col_blk(d_v), row_blk(d_v), row_blk(2 * d),
                 row_blk(N_GATE_COLS), col_blk(N_GATE_COLS),
                 col_blk(d_n), col_blk(A_HEADS * LANES), row_blk(d_n), col_blk(A_HEADS * A_DV),
                 row_blk(LANES)]
    in_specs = ([row_blk(d), tab_blk, tab_blk, tab_t_blk, tab_t_blk]
                + [_resident(w.shape) for w in weights])
    return pl.pallas_call(
        functools.partial(_in_proj_kernel, n_split=2 if tm % 256 == 0 else 1),
        grid=(rows // tm,),
        in_specs=in_specs,
        out_specs=out_specs,
        out_shape=out_shape,
        compiler_params=pltpu.CompilerParams(dimension_semantics=("parallel",),
                                             vmem_limit_bytes=VMEM_LIMIT),
        name="in_proj",
    )(x2d, cos4, sin4, cos4.T, sin4.T, *weights)


def _mlstm_kernel(qf_ref, kf_ref, vf_ref, gf_ref, gtf_ref,
                  qb_ref, kb_ref, vb_ref, gb_ref, gtb_ref,
                  km_ref, vm_ref, gtm_ref,
                  hf_ref, hb_ref, c_scr, m_scr):
    c_idx = pl.program_id(1)
    lc = kf_ref.shape[0]
    lm = km_ref.shape[0]
    n_pairs = M_HEADS // 2
    half = LANES // 2
    log2e = 1.4426950408889634

    def tri(n, lower):
        r = lax.broadcasted_iota(jnp.int32, (n, n), 0)
        c = lax.broadcasted_iota(jnp.int32, (n, n), 1)
        return (r >= c) if lower else (r <= c)

    def as_bf16(mask):
        return jnp.where(mask, 1.0, 0.0).astype(BF16)

    def eye(n):
        r = lax.broadcasted_iota(jnp.int32, (n, n), 0)
        c = lax.broadcasted_iota(jnp.int32, (n, n), 1)
        return as_bf16(r == c)

    eye_t = eye(lc)
    lane = lax.broadcasted_iota(jnp.int32, (1, LANES), 1)
    sub128 = lax.broadcasted_iota(jnp.int32, (LANES, 1), 0)

    def lane_mask(h):
        return (lane < half) if h % 2 == 0 else (lane >= half)

    def sublane_mask(h):
        return (sub128 < half) if h % 2 == 0 else (sub128 >= half)

    def split3(x):
        hi = x.astype(BF16)
        r1 = x - hi.astype(F32)
        mid = r1.astype(BF16)
        lo = (r1 - mid.astype(F32)).astype(BF16)
        return hi, mid, lo

    def cum_rows(g_rows, tri_b):
        hi, mid, lo = split3(g_rows)
        return _mm(hi, tri_b) + _mm(mid, tri_b) + _mm(lo, tri_b)

    def cum_cols(tri_b, g_cols):
        hi, mid, lo = split3(g_cols)
        return _mm(tri_b, hi) + _mm(tri_b, mid) + _mm(tri_b, lo)

    def state_update(direction, pair, k2, vt_pair, cum_row, g_rows, tot_col):
        n = k2.shape[0]
        d_c = None
        decays = []
        for sub in range(2):
            h = 2 * pair + sub
            j = direction * M_HEADS + h
            jf = 2 * M_HEADS + j
            m_prev = m_scr[j][0:1, 0:1]
            tot = tot_col[jf:jf + 1, :]
            g_row = tot - cum_row[jf:jf + 1, :] + g_rows[j:j + 1, :]
            m_new = jnp.maximum(tot + m_prev, jnp.max(g_row, axis=1, keepdims=True))
            decays.append(jnp.exp2(tot + m_prev - m_new))
            wk = jnp.exp2(g_row - m_new)
            vw = jnp.concatenate([(vt_pair[sub].astype(F32) * wk).astype(BF16),
                                  jnp.broadcast_to(wk.astype(BF16), (ONES_ROWS, n))], axis=0)
            km = jnp.where(lane_mask(h), k2, jnp.zeros_like(k2))
            upd = _mm(vw, km)
            d_c = upd if d_c is None else d_c + upd
            m_scr[j] = jnp.broadcast_to(m_new, m_scr.shape[1:])
        decay_cols = jnp.where(lane < half, decays[0], decays[1])
        ci = direction * n_pairs + pair
        c_scr[ci] = decay_cols * c_scr[ci] + d_c

    @pl.when(c_idx == 0)
    def _init():
        c_scr[...] = jnp.zeros_like(c_scr)
        m_scr[...] = jnp.zeros_like(m_scr)
        g_rows = gtm_ref[...] * log2e
        cum_m = cum_rows(g_rows, as_bf16(tri(lm, False)))
        tot_m = cum_m[:, lm - 1:lm]
        for pair in range(n_pairs):
            k2 = km_ref[:, pair * LANES:(pair + 1) * LANES]
            vt_pair = [vm_ref[(2 * pair + s) * M_DV:(2 * pair + s + 1) * M_DV, :] for s in range(2)]
            state_update(0, pair, k2, vt_pair, cum_m, g_rows, tot_m)

    views = ((qf_ref, kf_ref, vf_ref, gf_ref, gtf_ref, hf_ref),
             (qb_ref, kb_ref, vb_ref, gb_ref, gtb_ref, hb_ref))
    shared = []
    for direction, (_, _, _, g_ref, gt_ref, _) in enumerate(views):
        fwd = direction == 0
        g_cols = g_ref[...] * log2e
        g_rows = gt_ref[...] * log2e
        cum_row = cum_rows(g_rows, as_bf16(tri(lc, not fwd)))
        cum_col = cum_cols(as_bf16(tri(lc, fwd)), g_cols)
        tot_col = cum_row[:, lc - 1:lc] if fwd else cum_row[:, 0:1]
        key_cols = g_cols[:, :2 * M_HEADS] - cum_col[:, 2 * M_HEADS:]
        valid = tri(lc, not fwd)
        shared.append((g_rows, cum_row, tot_col, key_cols, valid))

    units = [(d, p, s) for d in range(2) for p in range(n_pairs) for s in range(2)]
    pair_vals = {}
    pair_out = {}
    vals = {}

    def scores(u):
        direction, pair, sub = units[u]
        q_ref, k_ref, v_ref = views[direction][:3]
        h = 2 * pair + sub
        if sub == 0:
            q2t = q_ref[pair * LANES:(pair + 1) * LANES, :]
            k2 = k_ref[:, pair * LANES:(pair + 1) * LANES]
            c_b = c_scr[direction * n_pairs + pair].astype(BF16)
            pair_vals[(direction, pair)] = (q2t, k2, c_b, [])
        q2t, k2, _, vt_pair = pair_vals[(direction, pair)]
        vt = v_ref[h * M_DV:(h + 1) * M_DV, :]
        vt_pair.append(vt)
        qmt = jnp.where(sublane_mask(h), q2t, jnp.zeros_like(q2t))
        vals[u] = dict(vt=vt, qmt=qmt, st=_mm(k2, qmt))

    def accumulate(u):
        direction, pair, sub = units[u]
        g_rows, cum_row, tot_col, key_cols, valid = shared[direction]
        _, k2, c_b, vt_pair = pair_vals[(direction, pair)]
        j = direction * M_HEADS + 2 * pair + sub
        jf = 2 * M_HEADS + j
        v = vals[u]
        b_row = cum_row[jf:jf + 1, :]
        dt = jnp.where(valid, key_cols[:, j:j + 1] + b_row, NEG)
        m_inter = b_row + m_scr[j][0:1, 0:1]
        m_t = jnp.maximum(m_inter, jnp.max(dt, axis=0, keepdims=True))
        wt = (jnp.exp2(dt - m_t) * v['st']).astype(BF16)
        qs = (v['qmt'].astype(F32) * jnp.exp2(m_inter - m_t)).astype(BF16)
        vext = jnp.concatenate([v['vt'], jnp.ones((ONES_ROWS, lc), BF16)], axis=0)
        v['acc'] = _mm(vext, wt) + _mm(c_b, qs)
        v['m_t'] = m_t
        if sub == 1:
            state_update(direction, pair, k2, vt_pair, cum_row, g_rows, tot_col)

    def finish(u):
        direction, pair, sub = units[u]
        h_ref = views[direction][5]
        v = vals.pop(u)
        den = jnp.maximum(jnp.abs(v['acc'][M_DV:M_DV + 1, :]), jnp.exp2(-v['m_t']))
        ht = (v['acc'][:M_DV, :] * (1.0 / den)).astype(BF16)
        if sub == 0:
            pair_out[(direction, pair)] = ht
        else:
            hts = jnp.concatenate([pair_out.pop((direction, pair)), ht], axis=0)
            h_ref[:, 2 * pair * M_DV:2 * (pair + 1) * M_DV] = _mm_nt(eye_t, hts).astype(h_ref.dtype)

    for t in range(len(units) + 2 * STAGE_SKEW):
        if t < len(units):
            scores(t)
        if 0 <= t - STAGE_SKEW < len(units):
            accumulate(t - STAGE_SKEW)
        if 0 <= t - 2 * STAGE_SKEW < len(units):
            finish(t - 2 * STAGE_SKEW)


def _mlstm(mq, mk, mv, gl, glt, meta, batch, lc):
    rows = mk.shape[0]
    seq = rows // batch
    nc = seq // lc
    km, vm, gtm = meta
    d_qk, d_v = M_HEADS * M_DQK, M_HEADS * M_DV

    fwd_row = lambda b, c: (b * nc + c, 0)
    bwd_row = lambda b, c: (b * nc + nc - 1 - c, 0)
    fwd_col = lambda b, c: (0, b * nc + c)
    bwd_col = lambda b, c: (0, b * nc + nc - 1 - c)

    def views(row_map, col_map):
        return [pl.BlockSpec((d_qk, lc), col_map), pl.BlockSpec((lc, d_qk), row_map),
                pl.BlockSpec((d_v, lc), col_map), pl.BlockSpec((lc, N_GATE_COLS), row_map),
                pl.BlockSpec((N_GATE_COLS, lc), col_map)]

    const = lambda a: pl.BlockSpec(a.shape, lambda b, c: (0, 0))
    in_specs = views(fwd_row, fwd_col) + views(bwd_row, bwd_col) + [const(a) for a in meta]
    return pl.pallas_call(
        _mlstm_kernel,
        grid=(batch, nc),
        in_specs=in_specs,
        out_specs=[pl.BlockSpec((lc, d_v), fwd_row), pl.BlockSpec((lc, d_v), bwd_row)],
        out_shape=[jax.ShapeDtypeStruct((rows, d_v), BF16)] * 2,
        scratch_shapes=[pltpu.VMEM((2 * (M_HEADS // 2), M_DV + ONES_ROWS, LANES), F32),
                        pltpu.VMEM((2 * M_HEADS, 8, LANES), F32)],
        compiler_params=pltpu.CompilerParams(dimension_semantics=("parallel", "arbitrary"),
                                             vmem_limit_bytes=VMEM_LIMIT),
        name="mlstm",
    )(mq, mk, mv, gl, glt, mq, mk, mv, gl, glt, km, vm, gtm)


def _attn_kernel(qnt_ref, qrt_ref, kn_ref, kr_ref, vt_ref, knm_ref, krm_ref, vtm_ref, o_ref, *, tk):
    qt = jnp.concatenate([qnt_ref[...], qrt_ref[...]], axis=0)

    def vext(vt):
        return jnp.concatenate([vt, jnp.ones((ONES_ROWS, vt.shape[1]), BF16)], axis=0)

    km = jnp.concatenate([knm_ref[...], krm_ref[...]], axis=1)
    s_meta = _mm(km, qt)
    key = lax.broadcasted_iota(jnp.int32, s_meta.shape, 0)
    s_meta = jnp.where(key < N_META, s_meta, NEG)
    n_chunks = kn_ref.shape[0] // tk

    def scores(j):
        rows = slice(j * tk, (j + 1) * tk)
        k = jnp.concatenate([kn_ref[rows, :], kr_ref[rows, :]], axis=1)
        return _mm(k, qt)

    ahead = 2
    pending = [scores(j) for j in range(min(ahead, n_chunks))]
    m = jnp.max(s_meta, axis=0, keepdims=True)
    acc = None
    for j in range(n_chunks):
        s = pending.pop(0)
        if j + ahead < n_chunks:
            pending.append(scores(j + ahead))
        m_new = jnp.maximum(m, jnp.max(s, axis=0, keepdims=True))
        p = jnp.exp2(s - m_new).astype(BF16)
        v = vext(vt_ref[:, j * tk:(j + 1) * tk])
        if j == 0:
            p = jnp.concatenate([jnp.exp2(s_meta - m_new).astype(BF16), p], axis=0)
            acc = _mm(jnp.concatenate([vext(vtm_ref[...]), v], axis=1), p)
        else:
            acc = jnp.exp2(m - m_new) * acc + _mm(v, p)
        m = m_new
    out_t = acc[:A_DV, :] * (1.0 / acc[A_DV:A_DV + 1, :])
    o_ref[...] = out_t.T.astype(o_ref.dtype)


def _attention(qnt, qrt, kn, kr, vat, meta, batch, tq, tk):
    rows = kn.shape[0]
    seq = rows // batch
    nq = seq // tq
    knm, krm, vatm = meta
    n_meta_pad = knm.shape[0]
    qt_blk = pl.BlockSpec((LANES, tq), lambda b, h, qi: (h, b * nq + qi))
    k_blk = pl.BlockSpec((seq, LANES), lambda b, h, qi: (b, h))
    kr_blk = pl.BlockSpec((seq, LANES), lambda b, h, qi: (b, 0))
    vt_blk = pl.BlockSpec((A_DV, seq), lambda b, h, qi: (h, b))
    meta_h = pl.BlockSpec((n_meta_pad, LANES), lambda b, h, qi: (0, h))
    meta_0 = pl.BlockSpec((n_meta_pad, LANES), lambda b, h, qi: (0, 0))
    meta_vt = pl.BlockSpec((A_DV, n_meta_pad), lambda b, h, qi: (h, 0))
    return pl.pallas_call(
        functools.partial(_attn_kernel, tk=tk),
        grid=(batch, A_HEADS, nq),
        in_specs=[qt_blk, qt_blk, k_blk, kr_blk, vt_blk, meta_h, meta_0, meta_vt],
        out_specs=pl.BlockSpec((tq, A_DV), lambda b, h, qi: (b * nq + qi, h)),
        out_shape=jax.ShapeDtypeStruct((rows, A_HEADS * A_DV), BF16),
        compiler_params=pltpu.CompilerParams(
            dimension_semantics=("parallel", "parallel", "parallel"),
            vmem_limit_bytes=VMEM_LIMIT),
        name="attention",
    )(qnt, qrt, kn, kr, vat, knm, krm, vatm)


def _mix_kernel(x_ref, hf_ref, hb_ref, so_ref, att_ref, sg_ref, gm_ref, wm_ref, wa_ref, wo_ref,
                gffn_ref, wrh_ref, wrl_ref, br_ref, h1_ref, u2_ref, idx_ref, gate_ref, hist_ref, *,
                n_split):
    d = x_ref.shape[1]
    tr = x_ref.shape[0] // n_split
    for part in range(n_split):
        rows = slice(part * tr, (part + 1) * tr)
        hs = hf_ref[rows, :].astype(F32) + hb_ref[rows, :].astype(F32)
        parts = []
        for h in range(M_HEADS):
            blk = hs[:, h * M_DV:(h + 1) * M_DV]
            parts.append(blk * lax.rsqrt(jnp.mean(blk * blk, axis=-1, keepdims=True) + EPS))
        hn = jnp.concatenate(parts, axis=1) * gm_ref[...] * so_ref[rows, :].astype(F32)
        y_m = _mm(hn.astype(BF16), wm_ref[...])
        y_a = _mm(att_ref[rows, :], wa_ref[...])
        sg = sg_ref[rows, :].astype(F32)
        mixed = sg[:, :d] * y_m + sg[:, d:] * y_a
        h1 = x_ref[rows, :] + _mm(mixed.astype(BF16), wo_ref[...])
        h1_ref[rows, :] = h1
        u2 = _rms(h1, gffn_ref[...])
        u2_hi = u2.astype(BF16)
        u2_ref[rows, :] = u2_hi
        u2_lo = (u2 - u2_hi.astype(F32)).astype(BF16)
        logits = (_mm(u2_hi, wrh_ref[...]) + _mm(u2_hi, wrl_ref[...]) + _mm(u2_lo, wrh_ref[...])
                  + br_ref[...])
        lane = lax.broadcasted_iota(jnp.int32, logits.shape, 1).astype(F32)
        out_lane = lax.broadcasted_iota(jnp.int32, (tr, TOP_K), 1)
        work = logits
        top_idx = jnp.zeros((tr, TOP_K), F32)
        top_exp = jnp.zeros((tr, TOP_K), F32)
        v_max = None
        hist = jnp.zeros((1, N_EXPERTS), F32)
        for k in range(TOP_K):
            v = jnp.max(work, axis=-1, keepdims=True)
            idx = jnp.min(jnp.where(work == v, lane, float(N_EXPERTS)), axis=-1, keepdims=True)
            picked = lane == idx
            work = jnp.where(picked, -jnp.inf, work)
            hist = hist + jnp.sum(jnp.where(picked, 1.0, 0.0), axis=0, keepdims=True)
            v_max = v if k == 0 else v_max
            top_idx = jnp.where(out_lane == k, idx, top_idx)
            top_exp = jnp.where(out_lane == k, jnp.exp(v - v_max), top_exp)
        idx_ref[rows, :] = top_idx.astype(jnp.int32)
        gate_ref[rows, :] = top_exp / jnp.sum(top_exp, axis=-1, keepdims=True)
        hist_ref[part] = hist


def _mix(x2d, hf, hb, so, att, sg, weights, tm):
    rows, d = x2d.shape
    row_blk = lambda w: pl.BlockSpec((tm, w), lambda i: (i, 0))
    in_specs = ([row_blk(d), row_blk(d), row_blk(d), row_blk(d), row_blk(d), row_blk(2 * d)]
                + [_resident(w.shape) for w in weights])
    n_split = 2 if tm % 32 == 0 else 1
    return pl.pallas_call(
        functools.partial(_mix_kernel, n_split=n_split),
        grid=(rows // tm,),
        in_specs=in_specs,
        out_specs=[row_blk(d), row_blk(d), row_blk(TOP_K), row_blk(TOP_K),
                   pl.BlockSpec((n_split, 1, N_EXPERTS), lambda i: (i, 0, 0))],
        out_shape=[jax.ShapeDtypeStruct((rows, d), F32), jax.ShapeDtypeStruct((rows, d), BF16),
                   jax.ShapeDtypeStruct((rows, TOP_K), jnp.int32),
                   jax.ShapeDtypeStruct((rows, TOP_K), F32),
                   jax.ShapeDtypeStruct((rows // tm * n_split, 1, N_EXPERTS), F32)],
        compiler_params=pltpu.CompilerParams(dimension_semantics=("parallel",),
                                             vmem_limit_bytes=VMEM_LIMIT),
        name="mix",
    )(x2d, hf, hb, so, att, sg, *weights)


def _expert_kernel(be_ref, first_ref, next_ref, slot_ref, nused_ref,
                   x_ref, w1_hbm, b1_ref, w2_hbm, b2_ref, y_ref,
                   w1f_scr, w2f_scr, w1b_scr, w2b_scr, sem):
    i = pl.program_id(0)

    def weight_copies(e, slot):
        return (pltpu.make_async_copy(w1_hbm.at[e], w1f_scr.at[slot], sem.at[0, slot]),
                pltpu.make_async_copy(w2_hbm.at[e], w2f_scr.at[slot], sem.at[1, slot]))

    @pl.when(i < nused_ref[0])
    def _():
        @pl.when(first_ref[i] == 1)
        def _switch():
            slot = slot_ref[i]

            @pl.when(i == 0)
            def _():
                for c in weight_copies(be_ref[i], slot):
                    c.start()

            @pl.when(next_ref[i] >= 0)
            def _():
                for c in weight_copies(next_ref[i], 1 - slot):
                    c.start()

            for c in weight_copies(be_ref[i], slot):
                c.wait()
            w1b_scr[...] = w1f_scr[slot].astype(BF16)
            w2b_scr[...] = w2f_scr[slot].astype(BF16)

        hid = _mm(x_ref[...], w1b_scr[...]) + b1_ref[...]
        glu = jnp.minimum(hid[:, :D_FF], SWIGLU_LIMIT)
        lin = jnp.clip(hid[:, D_FF:], -SWIGLU_LIMIT, SWIGLU_LIMIT)
        act = glu * _sigmoid(SWIGLU_ALPHA * glu) * (lin + 1.0)
        y_ref[...] = (_mm(act.astype(BF16), w2b_scr[...]) + b2_ref[...]).astype(y_ref.dtype)

    @pl.when(i >= nused_ref[0])
    def _():
        y_ref[...] = jnp.zeros_like(y_ref)


def _experts(block_e, n_used, xg, w1, b1, w2, b2, tme):
    rows, d = xg.shape
    nb = rows // tme
    i32 = jnp.int32
    used = jnp.arange(nb, dtype=i32) < n_used[0]
    first = jnp.concatenate([jnp.ones((1,), i32), (block_e[1:] != block_e[:-1]).astype(i32)])
    later = (block_e[None, :] > block_e[:, None]) & used[None, :]
    nxt = jnp.min(jnp.where(later, block_e[None, :], N_EXPERTS), axis=1)
    nxt = jnp.where(nxt == N_EXPERTS, -1, nxt).astype(i32)
    slot = ((jnp.cumsum(first) - 1) % 2).astype(i32)
    row_map = lambda i, *_: (i, 0)
    bias_map = lambda i, be, *_: (be[i], 0, 0)
    grid_spec = pltpu.PrefetchScalarGridSpec(
        num_scalar_prefetch=5,
        grid=(nb,),
        in_specs=[
            pl.BlockSpec((tme, d), row_map),
            pl.BlockSpec(memory_space=pl.ANY),
            pl.BlockSpec((None, 1, 2 * D_FF), bias_map),
            pl.BlockSpec(memory_space=pl.ANY),
            pl.BlockSpec((None, 1, d), bias_map),
        ],
        out_specs=pl.BlockSpec((tme, d), row_map),
        scratch_shapes=[pltpu.VMEM((2, d, 2 * D_FF), F32), pltpu.VMEM((2, D_FF, d), F32),
                        pltpu.VMEM((d, 2 * D_FF), BF16), pltpu.VMEM((D_FF, d), BF16),
                        pltpu.SemaphoreType.DMA((2, 2))],
    )
    return pl.pallas_call(
        _expert_kernel,
        grid_spec=grid_spec,
        out_shape=jax.ShapeDtypeStruct((rows, d), BF16),
        compiler_params=pltpu.CompilerParams(dimension_semantics=("arbitrary",),
                                             vmem_limit_bytes=VMEM_LIMIT),
        name="experts",
    )(block_e, first, nxt, slot, n_used, xg, w1, b1, w2, b2)


def _final_kernel(h1_ref, yg_ref, gate_ref, g_ref, o_ref):
    gates = gate_ref[...]
    y = h1_ref[...]
    for k in range(TOP_K):
        y = y + yg_ref[k].astype(F32) * gates[:, k:k + 1]
    o_ref[...] = _rms(y, g_ref[...])


def _final(h1, yg, gates, g, tm):
    rows, d = h1.shape
    row_blk = pl.BlockSpec((tm, d), lambda i: (i, 0))
    return pl.pallas_call(
        _final_kernel,
        grid=(rows // tm,),
        in_specs=[row_blk, pl.BlockSpec((TOP_K, tm, d), lambda i: (0, i, 0)),
                  pl.BlockSpec((tm, TOP_K), lambda i: (i, 0)), pl.BlockSpec((1, d), lambda i: (0, 0))],
        out_specs=row_blk,
        out_shape=jax.ShapeDtypeStruct((rows, d), F32),
        compiler_params=pltpu.CompilerParams(dimension_semantics=("parallel",)),
        name="final",
    )(h1, yg, gates, g)


def _rope_tables(pos):
    half = A_ROPE // 2
    freqs = ROPE_THETA ** (-jnp.arange(half, dtype=F32) / half)
    ang = pos.astype(F32)[:, None] * freqs[None, :]
    zeros = jnp.zeros((pos.shape[0], LANES - A_ROPE), F32)
    cos, sin = jnp.cos(ang), jnp.sin(ang)
    return jnp.concatenate([cos, cos, zeros], axis=1), jnp.concatenate([sin, sin, zeros], axis=1)


def _rope_pair(w):
    half = A_ROPE // 2
    zeros = jnp.zeros(w.shape[:-1] + (LANES - A_ROPE,), w.dtype)
    a = jnp.concatenate([w, zeros], axis=-1)
    b = jnp.concatenate([-w[..., half:], w[..., :half], zeros], axis=-1)
    return a, b


def _pick_tile(n, prefs):
    for t in prefs:
        if n % t == 0:
            return t
    return n


def kernel(x, meta_tokens, norm_mix, w_in, m_b_i, m_b_f, m_norm, w_m_proj, a_w_uq, a_w_ukv,
           a_norm_q, a_norm_kv, w_a_proj, w_out, norm_ffn, w_router, b_router, w1, b1, w2, b2,
           norm_final):
    assert norm_mix.shape[0] == 1, "single-layer encoder"
    batch, seq, d = x.shape
    rows = batch * seq
    x2d = x.reshape(rows, d)

    sizes = (M_HEADS * M_DQK, M_HEADS * M_DQK, M_HEADS * M_DV, M_HEADS * M_DV, 2 * M_HEADS,
             2 * M_HEADS, Q_LORA, KV_LORA, A_ROPE, 2 * d)
    offs = np.cumsum((0,) + sizes)
    w_q, w_k, w_v, w_o, w_i, w_f, w_cq, w_ckv, w_kr, w_g = (
        w_in[0][:, offs[i]:offs[i + 1]] for i in range(len(sizes)))
    kr_a, kr_b = _rope_pair(w_kr)
    w_qvt = jnp.concatenate([w_q * (M_DQK ** -0.5), w_v], axis=1).T.astype(BF16)
    w_ko = jnp.concatenate([w_k, w_o], axis=1).astype(BF16)
    w_lat = jnp.concatenate([w_cq, w_ckv, kr_a, kr_b], axis=1).astype(BF16)
    w_if = jnp.concatenate([w_i, w_f], axis=1).astype(BF16)
    b_if = jnp.concatenate([m_b_i[0].reshape(-1), m_b_f[0].reshape(-1)]).astype(F32)
    uq_a, uq_b = _rope_pair(a_w_uq[0][:, :, A_NOPE:])
    w_uqt = jnp.concatenate([a_w_uq[0][:, :, :A_NOPE].reshape(Q_LORA, -1), uq_a.reshape(Q_LORA, -1),
                             uq_b.reshape(Q_LORA, -1)], axis=1).T.astype(BF16)
    w_uk = a_w_ukv[0][:, :, :A_NOPE].reshape(KV_LORA, -1).astype(BF16)
    w_uvt = a_w_ukv[0][:, :, A_NOPE:].reshape(KV_LORA, -1).T.astype(BF16)
    in_weights = (norm_mix[0].reshape(1, d), w_qvt, w_ko, w_g.astype(BF16), w_lat,
                  w_if, w_if.T, b_if.reshape(1, -1), b_if.reshape(-1, 1),
                  a_norm_q[0].reshape(1, -1), a_norm_kv[0].reshape(1, -1), w_uqt, w_uk, w_uvt)

    cos_r, sin_r = _rope_tables(jnp.arange(N_META, N_META + seq))
    cos_m, sin_m = _rope_tables(jnp.arange(N_META))
    tm = _pick_tile(seq, (256, 128, 64, 32, 16))
    (mq, mk, mv, so, sg, gl, glt, qn, qr, kn, va, kr) = _in_proj(
        x2d, cos_r, sin_r, in_weights, _pick_tile(seq, (512, 256, 128, 64, 32, 16)))
    (_, mk_m, mvt_m, _, _, _, glt_m, _, _, kn_m, vat_m, kr_m) = _in_proj(
        meta_tokens.astype(x.dtype), cos_m, sin_m, in_weights, N_META)

    n_pad = LANES - N_META
    pad_gate = jnp.concatenate([jnp.full((n_pad, 2 * M_HEADS), NEG, F32),
                                jnp.zeros((n_pad, 2 * M_HEADS), F32)], axis=1)
    front = lambda a: jnp.concatenate([jnp.zeros((n_pad, a.shape[1]), a.dtype), a], axis=0)
    front_t = lambda a: jnp.concatenate([jnp.zeros((a.shape[0], n_pad), a.dtype), a], axis=1)
    mlstm_meta = (front(mk_m), front_t(mvt_m), jnp.concatenate([pad_gate.T, glt_m], axis=1))
    lc = _pick_tile(seq, (256, 128))
    hf, hb = _mlstm(mq, mk, mv, gl, glt, mlstm_meta, batch, lc)

    back = lambda a: jnp.concatenate([a, jnp.zeros((n_pad, a.shape[1]), a.dtype)], axis=0)
    back_t = lambda a: jnp.concatenate([a, jnp.zeros((a.shape[0], n_pad), a.dtype)], axis=1)
    tq = _pick_tile(seq, ATT_Q_TILES)
    tk = _pick_tile(seq, (512, 256, 128))
    att = _attention(qn, qr, kn, kr, va, (back(kn_m), back(kr_m), back_t(vat_m)), batch, tq, tk)

    wr_hi = w_router[0].astype(BF16)
    wr_lo = (w_router[0] - wr_hi.astype(F32)).astype(BF16)
    mix_weights = (m_norm[0].reshape(1, -1), w_m_proj[0].astype(BF16), w_a_proj[0].astype(BF16),
                   w_out[0].astype(BF16), norm_ffn[0].reshape(1, d), wr_hi, wr_lo,
                   b_router[0].reshape(1, -1))
    h1, u2, top_idx, gates, hist = _mix(x2d, hf, hb, so, att, sg, mix_weights,
                                  _pick_tile(rows, (512, 256, 128, 64, 32, 16)))

    tme = 512
    n = rows * TOP_K
    nb = -(-n // tme) + N_EXPERTS
    i32 = jnp.int32
    flat_e = top_idx.reshape(-1)
    pair_ids = jnp.arange(n, dtype=i32)
    _, order = lax.sort((flat_e, pair_ids), num_keys=1, is_stable=True)
    _, rank = lax.sort((order, pair_ids), num_keys=1)
    counts = jnp.sum(hist, axis=(0, 1)).astype(i32)
    start = jnp.cumsum(counts) - counts
    padded = (counts + tme - 1) // tme * tme
    padded_end = jnp.cumsum(padded)
    padded_start = padded_end - padded
    block_row0 = jnp.arange(nb, dtype=i32) * tme
    block_e = jnp.minimum(jnp.sum(padded_end[None, :] <= block_row0[:, None], axis=1, dtype=i32),
                          N_EXPERTS - 1)
    n_used = (padded_end[-1] // tme).astype(i32).reshape(1)
    off = (block_row0 - padded_start[block_e])[:, None] + jnp.arange(tme, dtype=i32)[None, :]
    valid = off < counts[block_e][:, None]
    src = jnp.clip(start[block_e][:, None] + off, 0, n - 1)
    filler = (block_row0[:, None] + jnp.arange(tme, dtype=i32)[None, :]) % rows
    buf_tok = jnp.where(valid, order[src] // TOP_K, filler).reshape(-1)
    slot = (padded_start - start)[flat_e] + rank
    xg = u2[buf_tok]
    yb = _experts(block_e, n_used, xg, w1[0], b1[0].reshape(N_EXPERTS, 1, -1), w2[0],
                  b2[0].reshape(N_EXPERTS, 1, -1), tme)
    yg = yb[slot.reshape(rows, TOP_K).T.reshape(-1)].reshape(TOP_K, rows, d)

    out = _final(h1, yg, gates, norm_final.reshape(1, d), tm)
    return out.reshape(batch, seq, d)
```

```python
import functools

import jax
import jax.numpy as jnp
import numpy as np
from jax import lax
from jax.experimental import pallas as pl
from jax.experimental.pallas import tpu as pltpu

N_META = 16
EPS = 1e-6
NEG = -1e30
M_HEADS = 8
M_DQK = 64
M_DV = 128
A_HEADS = 8
A_NOPE = 128
A_ROPE = 64
A_DV = 128
Q_LORA = 256
KV_LORA = 128
ROPE_THETA = 10000.0
N_EXPERTS = 32
TOP_K = 4
D_FF = 1024
SWIGLU_ALPHA = 1.702
SWIGLU_LIMIT = 7.0

LANES = 128
ATT_Q_TILES = (1024, 512, 256, 128)
STAGE_SKEW = 2
ONES_ROWS = 16
N_GATE_COLS = 4 * M_HEADS
ATT_SCALE = (A_NOPE + A_ROPE) ** -0.5 * 1.4426950408889634
VMEM_LIMIT = 56 * 1024 * 1024

F32 = jnp.float32
BF16 = jnp.bfloat16


def _mm(a, b):
    return jnp.dot(a, b, preferred_element_type=F32)


def _mm_nt(a, b):
    return lax.dot_general(a, b, (((1,), (1,)), ((), ())), preferred_element_type=F32)


def _mm_tn(a, b):
    return lax.dot_general(a, b, (((0,), (0,)), ((), ())), preferred_element_type=F32)


def _mm_exact(a, b):
    return jnp.dot(a, b, preferred_element_type=F32, precision=lax.Precision.HIGHEST)


def _rms(x, g):
    return x * lax.rsqrt(jnp.mean(x * x, axis=-1, keepdims=True) + EPS) * g


def _log_sigmoid(x):
    return jnp.minimum(x, 0.0) - jnp.log1p(jnp.exp(-jnp.abs(x)))


def _sigmoid(x):
    return 1.0 / (1.0 + jnp.exp(-x))


def _resident(shape):
    nd = len(shape)
    return pl.BlockSpec(shape, lambda *_: (0,) * nd, pipeline_mode=pl.Buffered(1))


def _in_proj_kernel(x_ref, cos_ref, sin_ref, cost_ref, sint_ref, gmix_ref, wqvt_ref, wko_ref, wgate_ref,
                    wlat_ref, wif_ref, wift_ref, bif_ref, bift_ref, gq_ref, gkv_ref, wuqt_ref, wuk_ref,
                    wuvt_ref, mqt_ref, mk_ref, mvt_ref, so_ref, sg_ref, gl_ref, glt_ref,
                    qnt_ref, qrt_ref, kn_ref, vat_ref, kr_ref, *, n_split):
    d_qk = M_HEADS * M_DQK
    d_n = A_HEADS * A_NOPE
    tr = x_ref.shape[0] // n_split
    for part in range(n_split):
        rows = slice(part * tr, (part + 1) * tr)
        ub = _rms(x_ref[rows, :], gmix_ref[...]).astype(BF16)

        qvt = _mm_nt(wqvt_ref[...], ub)
        mqt_ref[:, rows] = qvt[:d_qk, :].astype(BF16)
        mvt_ref[:, rows] = qvt[d_qk:, :].astype(BF16)
        ko = _mm(ub, wko_ref[...])
        mk_ref[rows, :] = ko[:, :d_qk].astype(BF16)
        so_ref[rows, :] = _sigmoid(ko[:, d_qk:]).astype(BF16)
        sg_ref[rows, :] = _sigmoid(_mm(ub, wgate_ref[...])).astype(BF16)

        gi = _mm(ub, wif_ref[...]) + bif_ref[...]
        col = lax.broadcasted_iota(jnp.int32, gi.shape, 1)
        gl_ref[rows, :] = jnp.where(col < 2 * M_HEADS, gi, _log_sigmoid(gi))
        git = _mm_nt(wift_ref[...], ub) + bift_ref[...]
        row = lax.broadcasted_iota(jnp.int32, git.shape, 0)
        glt_ref[:, rows] = jnp.where(row < 2 * M_HEADS, git, _log_sigmoid(git))

        lat = _mm(ub, wlat_ref[...])
        kr_ref[rows, :] = (lat[:, Q_LORA + KV_LORA:Q_LORA + KV_LORA + LANES] * cos_ref[rows, :]
                           + lat[:, Q_LORA + KV_LORA + LANES:] * sin_ref[rows, :]).astype(BF16)

        cqn = _rms(lat[:, :Q_LORA], gq_ref[...]).astype(BF16)
        q3t = _mm_nt(wuqt_ref[...], cqn)
        qnt_ref[:, rows] = (q3t[:d_n, :] * ATT_SCALE).astype(BF16)
        cos_t = cost_ref[:, rows]
        sin_t = sint_ref[:, rows]
        for h in range(A_HEADS):
            a = q3t[d_n + h * LANES:d_n + (h + 1) * LANES, :]
            b = q3t[d_n + (A_HEADS + h) * LANES:d_n + (A_HEADS + h + 1) * LANES, :]
            qrt_ref[h * LANES:(h + 1) * LANES, rows] = ((a * cos_t + b * sin_t) * ATT_SCALE).astype(BF16)

        ckvn = _rms(lat[:, Q_LORA:Q_LORA + KV_LORA], gkv_ref[...]).astype(BF16)
        kn_ref[rows, :] = _mm(ckvn, wuk_ref[...]).astype(BF16)
        vat_ref[:, rows] = _mm_nt(wuvt_ref[...], ckvn).astype(BF16)


def _in_proj(x2d, cos4, sin4, weights, tm):
    rows, d = x2d.shape
    n_tab = cos4.shape[0] // tm
    row_blk = lambda w: pl.BlockSpec((tm, w), lambda i: (i, 0))
    tab_blk = pl.BlockSpec((tm, LANES), lambda i: (i % n_tab, 0))
    tab_t_blk = pl.BlockSpec((LANES, tm), lambda i: (0, i % n_tab))
    d_qk, d_v, d_n = M_HEADS * M_DQK, M_HEADS * M_DV, A_HEADS * A_NOPE
    out_shape = [
        jax.ShapeDtypeStruct((d_qk, rows), BF16),
        jax.ShapeDtypeStruct((rows, d_qk), BF16),
        jax.ShapeDtypeStruct((d_v, rows), BF16),
        jax.ShapeDtypeStruct((rows, d_v), BF16),
        jax.ShapeDtypeStruct((rows, 2 * d), BF16),
        jax.ShapeDtypeStruct((rows, N_GATE_COLS), F32),
        jax.ShapeDtypeStruct((N_GATE_COLS, rows), F32),
        jax.ShapeDtypeStruct((d_n, rows), BF16),
        jax.ShapeDtypeStruct((A_HEADS * LANES, rows), BF16),
        jax.ShapeDtypeStruct((rows, d_n), BF16),
        jax.ShapeDtypeStruct((A_HEADS * A_DV, rows), BF16),
        jax.ShapeDtypeStruct((rows, LANES), BF16),
    ]
    col_blk = lambda h: pl.BlockSpec((h, tm), lambda i: (0, i))
    out_specs = [col_blk(d_qk), row_blk(d_qk), col_blk(d_v), row_blk(d_v), row_blk(2 * d),
                 row_blk(N_GATE_COLS), col_blk(N_GATE_COLS),
                 col_blk(d_n), col_blk(A_HEADS * LANES), row_blk(d_n), col_blk(A_HEADS * A_DV),
                 row_blk(LANES)]
    in_specs = ([row_blk(d), tab_blk, tab_blk, tab_t_blk, tab_t_blk]
                + [_resident(w.shape) for w in weights])
    return pl.pallas_call(
        functools.partial(_in_proj_kernel, n_split=2 if tm % 256 == 0 else 1),
        grid=(rows // tm,),
        in_specs=in_specs,
        out_specs=out_specs,
        out_shape=out_shape,
        compiler_params=pltpu.CompilerParams(dimension_semantics=("parallel",),
                                             vmem_limit_bytes=VMEM_LIMIT),
        name="in_proj",
    )(x2d, cos4, sin4, cos4.T, sin4.T, *weights)


def _mlstm_kernel(qf_ref, kf_ref, vf_ref, gf_ref, gtf_ref,
                  qb_ref, kb_ref, vb_ref, gb_ref, gtb_ref,
                  km_ref, vm_ref, gtm_ref,
                  hf_ref, hb_ref, c_scr, m_scr):
    c_idx = pl.program_id(1)
    lc = kf_ref.shape[0]
    lm = km_ref.shape[0]
    n_pairs = M_HEADS // 2
    half = LANES // 2
    log2e = 1.4426950408889634

    def tri(n, lower):
        r = lax.broadcasted_iota(jnp.int32, (n, n), 0)
        c = lax.broadcasted_iota(jnp.int32, (n, n), 1)
        return (r >= c) if lower else (r <= c)

    def as_bf16(mask):
        return jnp.where(mask, 1.0, 0.0).astype(BF16)

    def eye(n):
        r = lax.broadcasted_iota(jnp.int32, (n, n), 0)
        c = lax.broadcasted_iota(jnp.int32, (n, n), 1)
        return as_bf16(r == c)

    eye_t = eye(lc)
    lane = lax.broadcasted_iota(jnp.int32, (1, LANES), 1)
    sub128 = lax.broadcasted_iota(jnp.int32, (LANES, 1), 0)

    def lane_mask(h):
        return (lane < half) if h % 2 == 0 else (lane >= half)

    def sublane_mask(h):
        return (sub128 < half) if h % 2 == 0 else (sub128 >= half)

    def split3(x):
        hi = x.astype(BF16)
        r1 = x - hi.astype(F32)
        mid = r1.astype(BF16)
        lo = (r1 - mid.astype(F32)).astype(BF16)
        return hi, mid, lo

    def cum_rows(g_rows, tri_b):
        hi, mid, lo = split3(g_rows)
        return _mm(hi, tri_b) + _mm(mid, tri_b) + _mm(lo, tri_b)

    def cum_cols(tri_b, g_cols):
        hi, mid, lo = split3(g_cols)
        return _mm(tri_b, hi) + _mm(tri_b, mid) + _mm(tri_b, lo)

    def state_update(direction, pair, k2, vt_pair, cum_row, g_rows, tot_col):
        n = k2.shape[0]
        d_c = None
        decays = []
        for sub in range(2):
            h = 2 * pair + sub
            j = direction * M_HEADS + h
            jf = 2 * M_HEADS + j
            m_prev = m_scr[j][0:1, 0:1]
            tot = tot_col[jf:jf + 1, :]
            g_row = tot - cum_row[jf:jf + 1, :] + g_rows[j:j + 1, :]
            m_new = jnp.maximum(tot + m_prev, jnp.max(g_row, axis=1, keepdims=True))
            decays.append(jnp.exp2(tot + m_prev - m_new))
            wk = jnp.exp2(g_row - m_new)
            vw = jnp.concatenate([(vt_pair[sub].astype(F32) * wk).astype(BF16),
                                  jnp.broadcast_to(wk.astype(BF16), (ONES_ROWS, n))], axis=0)
            km = jnp.where(lane_mask(h), k2, jnp.zeros_like(k2))
            upd = _mm(vw, km)
            d_c = upd if d_c is None else d_c + upd
            m_scr[j] = jnp.broadcast_to(m_new, m_scr.shape[1:])
        decay_cols = jnp.where(lane < half, decays[0], decays[1])
        ci = direction * n_pairs + pair
        c_scr[ci] = decay_cols * c_scr[ci] + d_c

    @pl.when(c_idx == 0)
    def _init():
        c_scr[...] = jnp.zeros_like(c_scr)
        m_scr[...] = jnp.zeros_like(m_scr)
        g_rows = gtm_ref[...] * log2e
        cum_m = cum_rows(g_rows, as_bf16(tri(lm, False)))
        tot_m = cum_m[:, lm - 1:lm]
        for pair in range(n_pairs):
            k2 = km_ref[:, pair * LANES:(pair + 1) * LANES]
            vt_pair = [vm_ref[(2 * pair + s) * M_DV:(2 * pair + s + 1) * M_DV, :] for s in range(2)]
            state_update(0, pair, k2, vt_pair, cum_m, g_rows, tot_m)

    views = ((qf_ref, kf_ref, vf_ref, gf_ref, gtf_ref, hf_ref),
             (qb_ref, kb_ref, vb_ref, gb_ref, gtb_ref, hb_ref))
    shared = []
    for direction, (_, _, _, g_ref, gt_ref, _) in enumerate(views):
        fwd = direction == 0
        g_cols = g_ref[...] * log2e
        g_rows = gt_ref[...] * log2e
        cum_row = cum_rows(g_rows, as_bf16(tri(lc, not fwd)))
        cum_col = cum_cols(as_bf16(tri(lc, fwd)), g_cols)
        tot_col = cum_row[:, lc - 1:lc] if fwd else cum_row[:, 0:1]
        key_cols = g_cols[:, :2 * M_HEADS] - cum_col[:, 2 * M_HEADS:]
        valid = tri(lc, not fwd)
        shared.append((g_rows, cum_row, tot_col, key_cols, valid))

    units = [(d, p, s) for d in range(2) for p in range(n_pairs) for s in range(2)]
    pair_vals = {}
    pair_out = {}
    vals = {}

    def scores(u):
        direction, pair, sub = units[u]
        q_ref, k_ref, v_ref = views[direction][:3]
        h = 2 * pair + sub
        if sub == 0:
            q2t = q_ref[pair * LANES:(pair + 1) * LANES, :]
            k2 = k_ref[:, pair * LANES:(pair + 1) * LANES]
            c_b = c_scr[direction * n_pairs + pair].astype(BF16)
            pair_vals[(direction, pair)] = (q2t, k2, c_b, [])
        q2t, k2, _, vt_pair = pair_vals[(direction, pair)]
        vt = v_ref[h * M_DV:(h + 1) * M_DV, :]
        vt_pair.append(vt)
        qmt = jnp.where(sublane_mask(h), q2t, jnp.zeros_like(q2t))
        vals[u] = dict(vt=vt, qmt=qmt, st=_mm(k2, qmt))

    def accumulate(u):
        direction, pair, sub = units[u]
        g_rows, cum_row, tot_col, key_cols, valid = shared[direction]
        _, k2, c_b, vt_pair = pair_vals[(direction, pair)]
        j = direction * M_HEADS + 2 * pair + sub
        jf = 2 * M_HEADS + j
        v = vals[u]
        b_row = cum_row[jf:jf + 1, :]
        dt = jnp.where(valid, key_cols[:, j:j + 1] + b_row, NEG)
        m_inter = b_row + m_scr[j][0:1, 0:1]
        m_t = jnp.maximum(m_inter, jnp.max(dt, axis=0, keepdims=True))
        wt = (jnp.exp2(dt - m_t) * v['st']).astype(BF16)
        qs = (v['qmt'].astype(F32) * jnp.exp2(m_inter - m_t)).astype(BF16)
        vext = jnp.concatenate([v['vt'], jnp.ones((ONES_ROWS, lc), BF16)], axis=0)
        v['acc'] = _mm(vext, wt) + _mm(c_b, qs)
        v['m_t'] = m_t
        if sub == 1:
            state_update(direction, pair, k2, vt_pair, cum_row, g_rows, tot_col)

    def finish(u):
        direction, pair, sub = units[u]
        h_ref = views[direction][5]
        v = vals.pop(u)
        den = jnp.maximum(jnp.abs(v['acc'][M_DV:M_DV + 1, :]), jnp.exp2(-v['m_t']))
        ht = (v['acc'][:M_DV, :] * (1.0 / den)).astype(BF16)
        if sub == 0:
            pair_out[(direction, pair)] = ht
        else:
            hts = jnp.concatenate([pair_out.pop((direction, pair)), ht], axis=0)
            h_ref[:, 2 * pair * M_DV:2 * (pair + 1) * M_DV] = _mm_nt(eye_t, hts).astype(h_ref.dtype)

    for t in range(len(units) + 2 * STAGE_SKEW):
        if t < len(units):
            scores(t)
        if 0 <= t - STAGE_SKEW < len(units):
            accumulate(t - STAGE_SKEW)
        if 0 <= t - 2 * STAGE_SKEW < len(units):
            finish(t - 2 * STAGE_SKEW)


def _mlstm(mq, mk, mv, gl, glt, meta, batch, lc):
    rows = mk.shape[0]
    seq = rows // batch
    nc = seq // lc
    km, vm, gtm = meta
    d_qk, d_v = M_HEADS * M_DQK, M_HEADS * M_DV

    fwd_row = lambda b, c: (b * nc + c, 0)
    bwd_row = lambda b, c: (b * nc + nc - 1 - c, 0)
    fwd_col = lambda b, c: (0, b * nc + c)
    bwd_col = lambda b, c: (0, b * nc + nc - 1 - c)

    def views(row_map, col_map):
        return [pl.BlockSpec((d_qk, lc), col_map), pl.BlockSpec((lc, d_qk), row_map),
                pl.BlockSpec((d_v, lc), col_map), pl.BlockSpec((lc, N_GATE_COLS), row_map),
                pl.BlockSpec((N_GATE_COLS, lc), col_map)]

    const = lambda a: pl.BlockSpec(a.shape, lambda b, c: (0, 0))
    in_specs = views(fwd_row, fwd_col) + views(bwd_row, bwd_col) + [const(a) for a in meta]
    return pl.pallas_call(
        _mlstm_kernel,
        grid=(batch, nc),
        in_specs=in_specs,
        out_specs=[pl.BlockSpec((lc, d_v), fwd_row), pl.BlockSpec((lc, d_v), bwd_row)],
        out_shape=[jax.ShapeDtypeStruct((rows, d_v), BF16)] * 2,
        scratch_shapes=[pltpu.VMEM((2 * (M_HEADS // 2), M_DV + ONES_ROWS, LANES), F32),
                        pltpu.VMEM((2 * M_HEADS, 8, LANES), F32)],
        compiler_params=pltpu.CompilerParams(dimension_semantics=("parallel", "arbitrary"),
                                             vmem_limit_bytes=VMEM_LIMIT),
        name="mlstm",
    )(mq, mk, mv, gl, glt, mq, mk, mv, gl, glt, km, vm, gtm)


def _attn_kernel(qnt_ref, qrt_ref, kn_ref, kr_ref, vt_ref, knm_ref, krm_ref, vtm_ref, o_ref, *, tk):
    qt = jnp.concatenate([qnt_ref[...], qrt_ref[...]], axis=0)

    def vext(vt):
        return jnp.concatenate([vt, jnp.ones((ONES_ROWS, vt.shape[1]), BF16)], axis=0)

    km = jnp.concatenate([knm_ref[...], krm_ref[...]], axis=1)
    s_meta = _mm(km, qt)
    key = lax.broadcasted_iota(jnp.int32, s_meta.shape, 0)
    s_meta = jnp.where(key < N_META, s_meta, NEG)
    n_chunks = kn_ref.shape[0] // tk

    def scores(j):
        rows = slice(j * tk, (j + 1) * tk)
        k = jnp.concatenate([kn_ref[rows, :], kr_ref[rows, :]], axis=1)
        return _mm(k, qt)

    ahead = 2
    pending = [scores(j) for j in range(min(ahead, n_chunks))]
    m = jnp.max(s_meta, axis=0, keepdims=True)
    acc = None
    for j in range(n_chunks):
        s = pending.pop(0)
        if j + ahead < n_chunks:
            pending.append(scores(j + ahead))
        m_new = jnp.maximum(m, jnp.max(s, axis=0, keepdims=True))
        p = jnp.exp2(s - m_new).astype(BF16)
        v = vext(vt_ref[:, j * tk:(j + 1) * tk])
        if j == 0:
            p = jnp.concatenate([jnp.exp2(s_meta - m_new).astype(BF16), p], axis=0)
            acc = _mm(jnp.concatenate([vext(vtm_ref[...]), v], axis=1), p)
        else:
            acc = jnp.exp2(m - m_new) * acc + _mm(v, p)
        m = m_new
    out_t = acc[:A_DV, :] * (1.0 / acc[A_DV:A_DV + 1, :])
    o_ref[...] = out_t.T.astype(o_ref.dtype)


def _attention(qnt, qrt, kn, kr, vat, meta, batch, tq, tk):
    rows = kn.shape[0]
    seq = rows // batch
    nq = seq // tq
    knm, krm, vatm = meta
    n_meta_pad = knm.shape[0]
    qt_blk = pl.BlockSpec((LANES, tq), lambda b, h, qi: (h, b * nq + qi))
    k_blk = pl.BlockSpec((seq, LANES), lambda b, h, qi: (b, h))
    kr_blk = pl.BlockSpec((seq, LANES), lambda b, h, qi: (b, 0))
    vt_blk = pl.BlockSpec((A_DV, seq), lambda b, h, qi: (h, b))
    meta_h = pl.BlockSpec((n_meta_pad, LANES), lambda b, h, qi: (0, h))
    meta_0 = pl.BlockSpec((n_meta_pad, LANES), lambda b, h, qi: (0, 0))
    meta_vt = pl.BlockSpec((A_DV, n_meta_pad), lambda b, h, qi: (h, 0))
    return pl.pallas_call(
        functools.partial(_attn_kernel, tk=tk),
        grid=(batch, A_HEADS, nq),
        in_specs=[qt_blk, qt_blk, k_blk, kr_blk, vt_blk, meta_h, meta_0, meta_vt],
        out_specs=pl.BlockSpec((tq, A_DV), lambda b, h, qi: (b * nq + qi, h)),
        out_shape=jax.ShapeDtypeStruct((rows, A_HEADS * A_DV), BF16),
        compiler_params=pltpu.CompilerParams(
            dimension_semantics=("parallel", "parallel", "parallel"),
            vmem_limit_bytes=VMEM_LIMIT),
        name="attention",
    )(qnt, qrt, kn, kr, vat, knm, krm, vatm)


def _mix_kernel(x_ref, hf_ref, hb_ref, so_ref, att_ref, sg_ref, gm_ref, wm_ref, wa_ref, wo_ref,
                gffn_ref, wrh_ref, wrl_ref, br_ref, h1_ref, u2_ref, idx_ref, gate_ref, hist_ref, *,
                n_split):
    d = x_ref.shape[1]
    tr = x_ref.shape[0] // n_split
    for part in range(n_split):
        rows = slice(part * tr, (part + 1) * tr)
        hs = hf_ref[rows, :].astype(F32) + hb_ref[rows, :].astype(F32)
        parts = []
        for h in range(M_HEADS):
            blk = hs[:, h * M_DV:(h + 1) * M_DV]
            parts.append(blk * lax.rsqrt(jnp.mean(blk * blk, axis=-1, keepdims=True) + EPS))
        hn = jnp.concatenate(parts, axis=1) * gm_ref[...] * so_ref[rows, :].astype(F32)
        y_m = _mm(hn.astype(BF16), wm_ref[...])
        y_a = _mm(att_ref[rows, :], wa_ref[...])
        sg = sg_ref[rows, :].astype(F32)
        mixed = sg[:, :d] * y_m + sg[:, d:] * y_a
        h1 = x_ref[rows, :] + _mm(mixed.astype(BF16), wo_ref[...])
        h1_ref[rows, :] = h1
        u2 = _rms(h1, gffn_ref[...])
        u2_hi = u2.astype(BF16)
        u2_ref[rows, :] = u2_hi
        u2_lo = (u2 - u2_hi.astype(F32)).astype(BF16)
        logits = (_mm(u2_hi, wrh_ref[...]) + _mm(u2_hi, wrl_ref[...]) + _mm(u2_lo, wrh_ref[...])
                  + br_ref[...])
        lane = lax.broadcasted_iota(jnp.int32, logits.shape, 1).astype(F32)
        out_lane = lax.broadcasted_iota(jnp.int32, (tr, TOP_K), 1)
        work = logits
        top_idx = jnp.zeros((tr, TOP_K), F32)
        top_exp = jnp.zeros((tr, TOP_K), F32)
        v_max = None
        hist = jnp.zeros((1, N_EXPERTS), F32)
        for k in range(TOP_K):
            v = jnp.max(work, axis=-1, keepdims=True)
            idx = jnp.min(jnp.where(work == v, lane, float(N_EXPERTS)), axis=-1, keepdims=True)
            picked = lane == idx
            work = jnp.where(picked, -jnp.inf, work)
            hist = hist + jnp.sum(jnp.where(picked, 1.0, 0.0), axis=0, keepdims=True)
            v_max = v if k == 0 else v_max
            top_idx = jnp.where(out_lane == k, idx, top_idx)
            top_exp = jnp.where(out_lane == k, jnp.exp(v - v_max), top_exp)
        idx_ref[rows, :] = top_idx.astype(jnp.int32)
        gate_ref[rows, :] = top_exp / jnp.sum(top_exp, axis=-1, keepdims=True)
        hist_ref[part] = hist


def _mix(x2d, hf, hb, so, att, sg, weights, tm):
    rows, d = x2d.shape
    row_blk = lambda w: pl.BlockSpec((tm, w), lambda i: (i, 0))
    in_specs = ([row_blk(d), row_blk(d), row_blk(d), row_blk(d), row_blk(d), row_blk(2 * d)]
                + [_resident(w.shape) for w in weights])
    n_split = 2 if tm % 32 == 0 else 1
    return pl.pallas_call(
        functools.partial(_mix_kernel, n_split=n_split),
        grid=(rows // tm,),
        in_specs=in_specs,
        out_specs=[row_blk(d), row_blk(d), row_blk(TOP_K), row_blk(TOP_K),
                   pl.BlockSpec((n_split, 1, N_EXPERTS), lambda i: (i, 0, 0))],
        out_shape=[jax.ShapeDtypeStruct((rows, d), F32), jax.ShapeDtypeStruct((rows, d), BF16),
                   jax.ShapeDtypeStruct((rows, TOP_K), jnp.int32),
                   jax.ShapeDtypeStruct((rows, TOP_K), F32),
                   jax.ShapeDtypeStruct((rows // tm * n_split, 1, N_EXPERTS), F32)],
        compiler_params=pltpu.CompilerParams(dimension_semantics=("parallel",),
                                             vmem_limit_bytes=VMEM_LIMIT),
        name="mix",
    )(x2d, hf, hb, so, att, sg, *weights)


def _expert_kernel(be_ref, first_ref, next_ref, slot_ref, nused_ref,
                   x_ref, w1_hbm, b1_ref, w2_hbm, b2_ref, y_ref,
                   w1f_scr, w2f_scr, w1b_scr, w2b_scr, sem):
    i = pl.program_id(0)

    def weight_copies(e, slot):
        return (pltpu.make_async_copy(w1_hbm.at[e], w1f_scr.at[slot], sem.at[0, slot]),
                pltpu.make_async_copy(w2_hbm.at[e], w2f_scr.at[slot], sem.at[1, slot]))

    @pl.when(i < nused_ref[0])
    def _():
        @pl.when(first_ref[i] == 1)
        def _switch():
            slot = slot_ref[i]

            @pl.when(i == 0)
            def _():
                for c in weight_copies(be_ref[i], slot):
                    c.start()

            @pl.when(next_ref[i] >= 0)
            def _():
                for c in weight_copies(next_ref[i], 1 - slot):
                    c.start()

            for c in weight_copies(be_ref[i], slot):
                c.wait()
            w1b_scr[...] = w1f_scr[slot].astype(BF16)
            w2b_scr[...] = w2f_scr[slot].astype(BF16)

        hid = _mm(x_ref[...], w1b_scr[...]) + b1_ref[...]
        glu = jnp.minimum(hid[:, :D_FF], SWIGLU_LIMIT)
        lin = jnp.clip(hid[:, D_FF:], -SWIGLU_LIMIT, SWIGLU_LIMIT)
        act = glu * _sigmoid(SWIGLU_ALPHA * glu) * (lin + 1.0)
        y_ref[...] = (_mm(act.astype(BF16), w2b_scr[...]) + b2_ref[...]).astype(y_ref.dtype)

    @pl.when(i >= nused_ref[0])
    def _():
        y_ref[...] = jnp.zeros_like(y_ref)


def _experts(block_e, n_used, xg, w1, b1, w2, b2, tme):
    rows, d = xg.shape
    nb = rows // tme
    i32 = jnp.int32
    used = jnp.arange(nb, dtype=i32) < n_used[0]
    first = jnp.concatenate([jnp.ones((1,), i32), (block_e[1:] != block_e[:-1]).astype(i32)])
    later = (block_e[None, :] > block_e[:, None]) & used[None, :]
    nxt = jnp.min(jnp.where(later, block_e[None, :], N_EXPERTS), axis=1)
    nxt = jnp.where(nxt == N_EXPERTS, -1, nxt).astype(i32)
    slot = ((jnp.cumsum(first) - 1) % 2).astype(i32)
    row_map = lambda i, *_: (i, 0)
    bias_map = lambda i, be, *_: (be[i], 0, 0)
    grid_spec = pltpu.PrefetchScalarGridSpec(
        num_scalar_prefetch=5,
        grid=(nb,),
        in_specs=[
            pl.BlockSpec((tme, d), row_map),
            pl.BlockSpec(memory_space=pl.ANY),
            pl.BlockSpec((None, 1, 2 * D_FF), bias_map),
            pl.BlockSpec(memory_space=pl.ANY),
            pl.BlockSpec((None, 1, d), bias_map),
        ],
        out_specs=pl.BlockSpec((tme, d), row_map),
        scratch_shapes=[pltpu.VMEM((2, d, 2 * D_FF), F32), pltpu.VMEM((2, D_FF, d), F32),
                        pltpu.VMEM((d, 2 * D_FF), BF16), pltpu.VMEM((D_FF, d), BF16),
                        pltpu.SemaphoreType.DMA((2, 2))],
    )
    return pl.pallas_call(
        _expert_kernel,
        grid_spec=grid_spec,
        out_shape=jax.ShapeDtypeStruct((rows, d), BF16),
        compiler_params=pltpu.CompilerParams(dimension_semantics=("arbitrary",),
                                             vmem_limit_bytes=VMEM_LIMIT),
        name="experts",
    )(block_e, first, nxt, slot, n_used, xg, w1, b1, w2, b2)


def _final_kernel(h1_ref, gate_ref, g_ref, *refs, tiles_per_stream):
    yg_refs, o_ref = refs[:-1], refs[-1]
    stream = pl.program_id(0) // tiles_per_stream
    gates = gate_ref[...]
    y = h1_ref[...]
    for k in range(TOP_K):
        rows_k = yg_refs[0][k]
        for s in range(1, len(yg_refs)):
            rows_k = jnp.where(stream == s, yg_refs[s][k], rows_k)
        y = y + rows_k.astype(F32) * gates[:, k:k + 1]
    o_ref[...] = _rms(y, g_ref[...])


def _final(h1, ygs, gates, g, tm):
    rows, d = h1.shape
    tiles_per_stream = ygs[0].shape[1] // tm
    row_blk = pl.BlockSpec((tm, d), lambda i: (i, 0))

    def yg_blk(s):
        clamp = lambda i: jnp.clip(i - s * tiles_per_stream, 0, tiles_per_stream - 1)
        return pl.BlockSpec((TOP_K, tm, d), lambda i: (0, clamp(i), 0))

    return pl.pallas_call(
        functools.partial(_final_kernel, tiles_per_stream=tiles_per_stream),
        grid=(rows // tm,),
        in_specs=[row_blk, pl.BlockSpec((tm, TOP_K), lambda i: (i, 0)), pl.BlockSpec((1, d), lambda i: (0, 0))]
        + [yg_blk(s) for s in range(len(ygs))],
        out_specs=row_blk,
        out_shape=jax.ShapeDtypeStruct((rows, d), F32),
        compiler_params=pltpu.CompilerParams(dimension_semantics=("parallel",)),
        name="final",
    )(h1, gates, g, *ygs)


def _rope_tables(pos):
    half = A_ROPE // 2
    freqs = ROPE_THETA ** (-jnp.arange(half, dtype=F32) / half)
    ang = pos.astype(F32)[:, None] * freqs[None, :]
    zeros = jnp.zeros((pos.shape[0], LANES - A_ROPE), F32)
    cos, sin = jnp.cos(ang), jnp.sin(ang)
    return jnp.concatenate([cos, cos, zeros], axis=1), jnp.concatenate([sin, sin, zeros], axis=1)


def _rope_pair(w):
    half = A_ROPE // 2
    zeros = jnp.zeros(w.shape[:-1] + (LANES - A_ROPE,), w.dtype)
    a = jnp.concatenate([w, zeros], axis=-1)
    b = jnp.concatenate([-w[..., half:], w[..., :half], zeros], axis=-1)
    return a, b


def _pick_tile(n, prefs):
    for t in prefs:
        if n % t == 0:
            return t
    return n


def kernel(x, meta_tokens, norm_mix, w_in, m_b_i, m_b_f, m_norm, w_m_proj, a_w_uq, a_w_ukv,
           a_norm_q, a_norm_kv, w_a_proj, w_out, norm_ffn, w_router, b_router, w1, b1, w2, b2,
           norm_final):
    assert norm_mix.shape[0] == 1, "single-layer encoder"
    batch, seq, d = x.shape
    rows = batch * seq
    x2d = x.reshape(rows, d)

    sizes = (M_HEADS * M_DQK, M_HEADS * M_DQK, M_HEADS * M_DV, M_HEADS * M_DV, 2 * M_HEADS,
             2 * M_HEADS, Q_LORA, KV_LORA, A_ROPE, 2 * d)
    offs = np.cumsum((0,) + sizes)
    w_q, w_k, w_v, w_o, w_i, w_f, w_cq, w_ckv, w_kr, w_g = (
        w_in[0][:, offs[i]:offs[i + 1]] for i in range(len(sizes)))
    kr_a, kr_b = _rope_pair(w_kr)
    w_qvt = jnp.concatenate([w_q * (M_DQK ** -0.5), w_v], axis=1).T.astype(BF16)
    w_ko = jnp.concatenate([w_k, w_o], axis=1).astype(BF16)
    w_lat = jnp.concatenate([w_cq, w_ckv, kr_a, kr_b], axis=1).astype(BF16)
    w_if = jnp.concatenate([w_i, w_f], axis=1).astype(BF16)
    b_if = jnp.concatenate([m_b_i[0].reshape(-1), m_b_f[0].reshape(-1)]).astype(F32)
    uq_a, uq_b = _rope_pair(a_w_uq[0][:, :, A_NOPE:])
    w_uqt = jnp.concatenate([a_w_uq[0][:, :, :A_NOPE].reshape(Q_LORA, -1), uq_a.reshape(Q_LORA, -1),
                             uq_b.reshape(Q_LORA, -1)], axis=1).T.astype(BF16)
    w_uk = a_w_ukv[0][:, :, :A_NOPE].reshape(KV_LORA, -1).astype(BF16)
    w_uvt = a_w_ukv[0][:, :, A_NOPE:].reshape(KV_LORA, -1).T.astype(BF16)
    in_weights = (norm_mix[0].reshape(1, d), w_qvt, w_ko, w_g.astype(BF16), w_lat,
                  w_if, w_if.T, b_if.reshape(1, -1), b_if.reshape(-1, 1),
                  a_norm_q[0].reshape(1, -1), a_norm_kv[0].reshape(1, -1), w_uqt, w_uk, w_uvt)

    cos_r, sin_r = _rope_tables(jnp.arange(N_META, N_META + seq))
    cos_m, sin_m = _rope_tables(jnp.arange(N_META))
    tm = _pick_tile(seq, (256, 128, 64, 32, 16))
    (mq, mk, mv, so, sg, gl, glt, qn, qr, kn, va, kr) = _in_proj(
        x2d, cos_r, sin_r, in_weights, _pick_tile(seq, (512, 256, 128, 64, 32, 16)))
    (_, mk_m, mvt_m, _, _, _, glt_m, _, _, kn_m, vat_m, kr_m) = _in_proj(
        meta_tokens.astype(x.dtype), cos_m, sin_m, in_weights, N_META)

    n_pad = LANES - N_META
    pad_gate = jnp.concatenate([jnp.full((n_pad, 2 * M_HEADS), NEG, F32),
                                jnp.zeros((n_pad, 2 * M_HEADS), F32)], axis=1)
    front = lambda a: jnp.concatenate([jnp.zeros((n_pad, a.shape[1]), a.dtype), a], axis=0)
    front_t = lambda a: jnp.concatenate([jnp.zeros((a.shape[0], n_pad), a.dtype), a], axis=1)
    mlstm_meta = (front(mk_m), front_t(mvt_m), jnp.concatenate([pad_gate.T, glt_m], axis=1))
    lc = _pick_tile(seq, (256, 128))
    hf, hb = _mlstm(mq, mk, mv, gl, glt, mlstm_meta, batch, lc)

    back = lambda a: jnp.concatenate([a, jnp.zeros((n_pad, a.shape[1]), a.dtype)], axis=0)
    back_t = lambda a: jnp.concatenate([a, jnp.zeros((a.shape[0], n_pad), a.dtype)], axis=1)
    tq = _pick_tile(seq, ATT_Q_TILES)
    tk = _pick_tile(seq, (512, 256, 128))
    att = _attention(qn, qr, kn, kr, va, (back(kn_m), back(kr_m), back_t(vat_m)), batch, tq, tk)

    wr_hi = w_router[0].astype(BF16)
    wr_lo = (w_router[0] - wr_hi.astype(F32)).astype(BF16)
    mix_weights = (m_norm[0].reshape(1, -1), w_m_proj[0].astype(BF16), w_a_proj[0].astype(BF16),
                   w_out[0].astype(BF16), norm_ffn[0].reshape(1, d), wr_hi, wr_lo,
                   b_router[0].reshape(1, -1))
    mix_tile = _pick_tile(rows, (512, 256, 128, 64, 32, 16))
    h1, u2, top_idx, gates, hist = _mix(x2d, hf, hb, so, att, sg, mix_weights, mix_tile)

    tme = 512
    i32 = jnp.int32
    n_streams = 2 if rows % (2 * mix_tile) == 0 else 1
    rs = rows // n_streams
    n = rs * TOP_K
    nb = -(-n // tme) + N_EXPERTS
    hist = hist.reshape(n_streams, -1, N_EXPERTS)
    b1r, b2r = b1[0].reshape(N_EXPERTS, 1, -1), b2[0].reshape(N_EXPERTS, 1, -1)
    ygs = []
    for s in range(n_streams):
        row0 = s * rs
        flat_e = top_idx[row0:row0 + rs].T.reshape(-1)
        pair_ids = jnp.arange(n, dtype=i32)
        _, order = lax.sort((flat_e, pair_ids), num_keys=1, is_stable=True)
        _, rank = lax.sort((order, pair_ids), num_keys=1)
        counts = jnp.sum(hist[s], axis=0).astype(i32)
        start = jnp.cumsum(counts) - counts
        padded = (counts + tme - 1) // tme * tme
        padded_end = jnp.cumsum(padded)
        padded_start = padded_end - padded
        block_row0 = jnp.arange(nb, dtype=i32) * tme
        block_e = jnp.minimum(jnp.sum(padded_end[None, :] <= block_row0[:, None], axis=1, dtype=i32),
                              N_EXPERTS - 1)
        n_used = (padded_end[-1] // tme).astype(i32).reshape(1)
        off = (block_row0 - padded_start[block_e])[:, None] + jnp.arange(tme, dtype=i32)[None, :]
        valid = off < counts[block_e][:, None]
        src = jnp.clip(start[block_e][:, None] + off, 0, n - 1)
        filler = (block_row0[:, None] + jnp.arange(tme, dtype=i32)[None, :]) % rs
        buf_tok = (jnp.where(valid, order[src] % rs, filler) + row0).reshape(-1)
        run_end = start + counts
        slot = rank + jnp.sum(jnp.where(rank[None, :] >= run_end[:, None], (padded - counts)[:, None], 0),
                              axis=0, dtype=i32)
        yb = _experts(block_e, n_used, u2[buf_tok], w1[0], b1r, w2[0], b2r, tme)
        ygs.append(yb[slot].reshape(TOP_K, rs, d))

    out = _final(h1, ygs, gates, norm_final.reshape(1, d), tm)
    return out.reshape(batch, seq, d)
```

```python
import functools

import jax
import jax.numpy as jnp
import numpy as np
from jax import lax
from jax.experimental import pallas as pl
from jax.experimental.pallas import tpu as pltpu

N_META = 16
EPS = 1e-6
NEG = -1e30
M_HEADS = 8
M_DQK = 64
M_DV = 128
A_HEADS = 8
A_NOPE = 128
A_ROPE = 64
A_DV = 128
Q_LORA = 256
KV_LORA = 128
ROPE_THETA = 10000.0
N_EXPERTS = 32
TOP_K = 4
D_FF = 1024
SWIGLU_ALPHA = 1.702
SWIGLU_LIMIT = 7.0

LANES = 128
ATT_Q_TILES = (1024, 512, 256, 128)
TOKEN_STREAMS = 1
STAGE_SKEW = 2
ONES_ROWS = 16
N_GATE_COLS = 4 * M_HEADS
ATT_SCALE = (A_NOPE + A_ROPE) ** -0.5 * 1.4426950408889634
VMEM_LIMIT = 56 * 1024 * 1024

F32 = jnp.float32
BF16 = jnp.bfloat16


def _mm(a, b):
    return jnp.dot(a, b, preferred_element_type=F32)


def _mm_nt(a, b):
    return lax.dot_general(a, b, (((1,), (1,)), ((), ())), preferred_element_type=F32)


def _mm_tn(a, b):
    return lax.dot_general(a, b, (((0,), (0,)), ((), ())), preferred_element_type=F32)


def _mm_exact(a, b):
    return jnp.dot(a, b, preferred_element_type=F32, precision=lax.Precision.HIGHEST)


def _rms(x, g):
    return x * lax.rsqrt(jnp.mean(x * x, axis=-1, keepdims=True) + EPS) * g


def _log_sigmoid(x):
    return jnp.minimum(x, 0.0) - jnp.log1p(jnp.exp(-jnp.abs(x)))


def _sigmoid(x):
    return 1.0 / (1.0 + jnp.exp(-x))


def _resident(shape):
    nd = len(shape)
    return pl.BlockSpec(shape, lambda *_: (0,) * nd, pipeline_mode=pl.Buffered(1))


def _in_proj_kernel(x_ref, cos_ref, sin_ref, cost_ref, sint_ref, gmix_ref, wqvt_ref, wko_ref, wgate_ref,
                    wlat_ref, wif_ref, wift_ref, bif_ref, bift_ref, gq_ref, gkv_ref, wuqt_ref, wuk_ref,
                    wuvt_ref, mqt_ref, mk_ref, mvt_ref, so_ref, sg_ref, gl_ref, glt_ref,
                    qnt_ref, qrt_ref, kn_ref, vat_ref, kr_ref, *, n_split):
    d_qk = M_HEADS * M_DQK
    d_n = A_HEADS * A_NOPE
    tr = x_ref.shape[0] // n_split
    for part in range(n_split):
        rows = slice(part * tr, (part + 1) * tr)
        ub = _rms(x_ref[rows, :], gmix_ref[...]).astype(BF16)

        qvt = _mm_nt(wqvt_ref[...], ub)
        mqt_ref[:, rows] = qvt[:d_qk, :].astype(BF16)
        mvt_ref[:, rows] = qvt[d_qk:, :].astype(BF16)
        ko = _mm(ub, wko_ref[...])
        mk_ref[rows, :] = ko[:, :d_qk].astype(BF16)
        so_ref[rows, :] = _sigmoid(ko[:, d_qk:]).astype(BF16)
        sg_ref[rows, :] = _sigmoid(_mm(ub, wgate_ref[...])).astype(BF16)

        gi = _mm(ub, wif_ref[...]) + bif_ref[...]
        col = lax.broadcasted_iota(jnp.int32, gi.shape, 1)
        gl_ref[rows, :] = jnp.where(col < 2 * M_HEADS, gi, _log_sigmoid(gi))
        git = _mm_nt(wift_ref[...], ub) + bift_ref[...]
        row = lax.broadcasted_iota(jnp.int32, git.shape, 0)
        glt_ref[:, rows] = jnp.where(row < 2 * M_HEADS, git, _log_sigmoid(git))

        lat = _mm(ub, wlat_ref[...])
        kr_ref[rows, :] = (lat[:, Q_LORA + KV_LORA:Q_LORA + KV_LORA + LANES] * cos_ref[rows, :]
                           + lat[:, Q_LORA + KV_LORA + LANES:] * sin_ref[rows, :]).astype(BF16)

        cqn = _rms(lat[:, :Q_LORA], gq_ref[...]).astype(BF16)
        q3t = _mm_nt(wuqt_ref[...], cqn)
        qnt_ref[:, rows] = (q3t[:d_n, :] * ATT_SCALE).astype(BF16)
        cos_t = cost_ref[:, rows]
        sin_t = sint_ref[:, rows]
        for h in range(A_HEADS):
            a = q3t[d_n + h * LANES:d_n + (h + 1) * LANES, :]
            b = q3t[d_n + (A_HEADS + h) * LANES:d_n + (A_HEADS + h + 1) * LANES, :]
            qrt_ref[h * LANES:(h + 1) * LANES, rows] = ((a * cos_t + b * sin_t) * ATT_SCALE).astype(BF16)

        ckvn = _rms(lat[:, Q_LORA:Q_LORA + KV_LORA], gkv_ref[...]).astype(BF16)
        kn_ref[rows, :] = _mm(ckvn, wuk_ref[...]).astype(BF16)
        vat_ref[:, rows] = _mm_nt(wuvt_ref[...], ckvn).astype(BF16)


def _in_proj(x2d, cos4, sin4, weights, tm):
    rows, d = x2d.shape
    n_tab = cos4.shape[0] // tm
    row_blk = lambda w: pl.BlockSpec((tm, w), lambda i: (i, 0))
    tab_blk = pl.BlockSpec((tm, LANES), lambda i: (i % n_tab, 0))
    tab_t_blk = pl.BlockSpec((LANES, tm), lambda i: (0, i % n_tab))
    d_qk, d_v, d_n = M_HEADS * M_DQK, M_HEADS * M_DV, A_HEADS * A_NOPE
    out_shape = [
        jax.ShapeDtypeStruct((d_qk, rows), BF16),
        jax.ShapeDtypeStruct((rows, d_qk), BF16),
        jax.ShapeDtypeStruct((d_v, rows), BF16),
        jax.ShapeDtypeStruct((rows, d_v), BF16),
        jax.ShapeDtypeStruct((rows, 2 * d), BF16),
        jax.ShapeDtypeStruct((rows, N_GATE_COLS), F32),
        jax.ShapeDtypeStruct((N_GATE_COLS, rows), F32),
        jax.ShapeDtypeStruct((d_n, rows), BF16),
        jax.ShapeDtypeStruct((A_HEADS * LANES, rows), BF16),
        jax.ShapeDtypeStruct((rows, d_n), BF16),
        jax.ShapeDtypeStruct((A_HEADS * A_DV, rows), BF16),
        jax.ShapeDtypeStruct((rows, LANES), BF16),
    ]
    col_blk = lambda h: pl.BlockSpec((h, tm), lambda i: (0, i))
    out_specs = [col_blk(d_qk), row_blk(d_qk), col_blk(d_v), row_blk(d_v), row_blk(2 * d),
                 row_blk(N_GATE_COLS), col_blk(N_GATE_COLS),
                 col_blk(d_n), col_blk(A_HEADS * LANES), row_blk(d_n), col_blk(A_HEADS * A_DV),
                 row_blk(LANES)]
    in_specs = ([row_blk(d), tab_blk, tab_blk, tab_t_blk, tab_t_blk]
                + [_resident(w.shape) for w in weights])
    return pl.pallas_call(
        functools.partial(_in_proj_kernel, n_split=2 if tm % 256 == 0 else 1),
        grid=(rows // tm,),
        in_specs=in_specs,
        out_specs=out_specs,
        out_shape=out_shape,
        compiler_params=pltpu.CompilerParams(dimension_semantics=("parallel",),
                                             vmem_limit_bytes=VMEM_LIMIT),
        name="in_proj",
    )(x2d, cos4, sin4, cos4.T, sin4.T, *weights)


def _mlstm_kernel(qf_ref, kf_ref, vf_ref, gf_ref, gtf_ref,
                  qb_ref, kb_ref, vb_ref, gb_ref, gtb_ref,
                  km_ref, vm_ref, gtm_ref,
                  hf_ref, hb_ref, c_scr, m_scr):
    c_idx = pl.program_id(1)
    lc = kf_ref.shape[0]
    lm = km_ref.shape[0]
    n_pairs = M_HEADS // 2
    half = LANES // 2
    log2e = 1.4426950408889634

    def tri(n, lower):
        r = lax.broadcasted_iota(jnp.int32, (n, n), 0)
        c = lax.broadcasted_iota(jnp.int32, (n, n), 1)
        return (r >= c) if lower else (r <= c)

    def as_bf16(mask):
        return jnp.where(mask, 1.0, 0.0).astype(BF16)

    def eye(n):
        r = lax.broadcasted_iota(jnp.int32, (n, n), 0)
        c = lax.broadcasted_iota(jnp.int32, (n, n), 1)
        return as_bf16(r == c)

    eye_t = eye(lc)
    lane = lax.broadcasted_iota(jnp.int32, (1, LANES), 1)
    sub128 = lax.broadcasted_iota(jnp.int32, (LANES, 1), 0)

    def lane_mask(h):
        return (lane < half) if h % 2 == 0 else (lane >= half)

    def sublane_mask(h):
        return (sub128 < half) if h % 2 == 0 else (sub128 >= half)

    def split3(x):
        hi = x.astype(BF16)
        r1 = x - hi.astype(F32)
        mid = r1.astype(BF16)
        lo = (r1 - mid.astype(F32)).astype(BF16)
        return hi, mid, lo

    def cum_rows(g_rows, tri_b):
        hi, mid, lo = split3(g_rows)
        return _mm(hi, tri_b) + _mm(mid, tri_b) + _mm(lo, tri_b)

    def cum_cols(tri_b, g_cols):
        hi, mid, lo = split3(g_cols)
        return _mm(tri_b, hi) + _mm(tri_b, mid) + _mm(tri_b, lo)

    def state_update(direction, pair, k2, vt_pair, cum_row, g_rows, tot_col):
        n = k2.shape[0]
        d_c = None
        decays = []
        for sub in range(2):
            h = 2 * pair + sub
            j = direction * M_HEADS + h
            jf = 2 * M_HEADS + j
            m_prev = m_scr[j][0:1, 0:1]
            tot = tot_col[jf:jf + 1, :]
            g_row = tot - cum_row[jf:jf + 1, :] + g_rows[j:j + 1, :]
            m_new = jnp.maximum(tot + m_prev, jnp.max(g_row, axis=1, keepdims=True))
            decays.append(jnp.exp2(tot + m_prev - m_new))
            wk = jnp.exp2(g_row - m_new)
            vw = jnp.concatenate([(vt_pair[sub].astype(F32) * wk).astype(BF16),
                                  jnp.broadcast_to(wk.astype(BF16), (ONES_ROWS, n))], axis=0)
            km = jnp.where(lane_mask(h), k2, jnp.zeros_like(k2))
            upd = _mm(vw, km)
            d_c = upd if d_c is None else d_c + upd
            m_scr[j] = jnp.broadcast_to(m_new, m_scr.shape[1:])
        decay_cols = jnp.where(lane < half, decays[0], decays[1])
        ci = direction * n_pairs + pair
        c_scr[ci] = decay_cols * c_scr[ci] + d_c

    @pl.when(c_idx == 0)
    def _init():
        c_scr[...] = jnp.zeros_like(c_scr)
        m_scr[...] = jnp.zeros_like(m_scr)
        g_rows = gtm_ref[...] * log2e
        cum_m = cum_rows(g_rows, as_bf16(tri(lm, False)))
        tot_m = cum_m[:, lm - 1:lm]
        for pair in range(n_pairs):
            k2 = km_ref[:, pair * LANES:(pair + 1) * LANES]
            vt_pair = [vm_ref[(2 * pair + s) * M_DV:(2 * pair + s + 1) * M_DV, :] for s in range(2)]
            state_update(0, pair, k2, vt_pair, cum_m, g_rows, tot_m)

    views = ((qf_ref, kf_ref, vf_ref, gf_ref, gtf_ref, hf_ref),
             (qb_ref, kb_ref, vb_ref, gb_ref, gtb_ref, hb_ref))
    shared = []
    for direction, (_, _, _, g_ref, gt_ref, _) in enumerate(views):
        fwd = direction == 0
        g_cols = g_ref[...] * log2e
        g_rows = gt_ref[...] * log2e
        cum_row = cum_rows(g_rows, as_bf16(tri(lc, not fwd)))
        cum_col = cum_cols(as_bf16(tri(lc, fwd)), g_cols)
        tot_col = cum_row[:, lc - 1:lc] if fwd else cum_row[:, 0:1]
        key_cols = g_cols[:, :2 * M_HEADS] - cum_col[:, 2 * M_HEADS:]
        valid = tri(lc, not fwd)
        shared.append((g_rows, cum_row, tot_col, key_cols, valid))

    units = [(d, p, s) for d in range(2) for p in range(n_pairs) for s in range(2)]
    pair_vals = {}
    pair_out = {}
    vals = {}

    def scores(u):
        direction, pair, sub = units[u]
        q_ref, k_ref, v_ref = views[direction][:3]
        h = 2 * pair + sub
        if sub == 0:
            q2t = q_ref[pair * LANES:(pair + 1) * LANES, :]
            k2 = k_ref[:, pair * LANES:(pair + 1) * LANES]
            c_b = c_scr[direction * n_pairs + pair].astype(BF16)
            pair_vals[(direction, pair)] = (q2t, k2, c_b, [])
        q2t, k2, _, vt_pair = pair_vals[(direction, pair)]
        vt = v_ref[h * M_DV:(h + 1) * M_DV, :]
        vt_pair.append(vt)
        qmt = jnp.where(sublane_mask(h), q2t, jnp.zeros_like(q2t))
        vals[u] = dict(vt=vt, qmt=qmt, st=_mm(k2, qmt))

    def accumulate(u):
        direction, pair, sub = units[u]
        g_rows, cum_row, tot_col, key_cols, valid = shared[direction]
        _, k2, c_b, vt_pair = pair_vals[(direction, pair)]
        j = direction * M_HEADS + 2 * pair + sub
        jf = 2 * M_HEADS + j
        v = vals[u]
        b_row = cum_row[jf:jf + 1, :]
        dt = jnp.where(valid, key_cols[:, j:j + 1] + b_row, NEG)
        m_inter = b_row + m_scr[j][0:1, 0:1]
        m_t = jnp.maximum(m_inter, jnp.max(dt, axis=0, keepdims=True))
        wt = (jnp.exp2(dt - m_t) * v['st']).astype(BF16)
        qs = (v['qmt'].astype(F32) * jnp.exp2(m_inter - m_t)).astype(BF16)
        vext = jnp.concatenate([v['vt'], jnp.ones((ONES_ROWS, lc), BF16)], axis=0)
        v['acc'] = _mm(vext, wt) + _mm(c_b, qs)
        v['m_t'] = m_t
        if sub == 1:
            state_update(direction, pair, k2, vt_pair, cum_row, g_rows, tot_col)

    def finish(u):
        direction, pair, sub = units[u]
        h_ref = views[direction][5]
        v = vals.pop(u)
        den = jnp.maximum(jnp.abs(v['acc'][M_DV:M_DV + 1, :]), jnp.exp2(-v['m_t']))
        ht = (v['acc'][:M_DV, :] * (1.0 / den)).astype(BF16)
        if sub == 0:
            pair_out[(direction, pair)] = ht
        else:
            hts = jnp.concatenate([pair_out.pop((direction, pair)), ht], axis=0)
            h_ref[:, 2 * pair * M_DV:2 * (pair + 1) * M_DV] = _mm_nt(eye_t, hts).astype(h_ref.dtype)

    for t in range(len(units) + 2 * STAGE_SKEW):
        if t < len(units):
            scores(t)
        if 0 <= t - STAGE_SKEW < len(units):
            accumulate(t - STAGE_SKEW)
        if 0 <= t - 2 * STAGE_SKEW < len(units):
            finish(t - 2 * STAGE_SKEW)


def _mlstm(mq, mk, mv, gl, glt, meta, batch, lc):
    rows = mk.shape[0]
    seq = rows // batch
    nc = seq // lc
    km, vm, gtm = meta
    d_qk, d_v = M_HEADS * M_DQK, M_HEADS * M_DV

    fwd_row = lambda b, c: (b * nc + c, 0)
    bwd_row = lambda b, c: (b * nc + nc - 1 - c, 0)
    fwd_col = lambda b, c: (0, b * nc + c)
    bwd_col = lambda b, c: (0, b * nc + nc - 1 - c)

    def views(row_map, col_map):
        return [pl.BlockSpec((d_qk, lc), col_map), pl.BlockSpec((lc, d_qk), row_map),
                pl.BlockSpec((d_v, lc), col_map), pl.BlockSpec((lc, N_GATE_COLS), row_map),
                pl.BlockSpec((N_GATE_COLS, lc), col_map)]

    const = lambda a: pl.BlockSpec(a.shape, lambda b, c: (0, 0))
    in_specs = views(fwd_row, fwd_col) + views(bwd_row, bwd_col) + [const(a) for a in meta]
    return pl.pallas_call(
        _mlstm_kernel,
        grid=(batch, nc),
        in_specs=in_specs,
        out_specs=[pl.BlockSpec((lc, d_v), fwd_row), pl.BlockSpec((lc, d_v), bwd_row)],
        out_shape=[jax.ShapeDtypeStruct((rows, d_v), BF16)] * 2,
        scratch_shapes=[pltpu.VMEM((2 * (M_HEADS // 2), M_DV + ONES_ROWS, LANES), F32),
                        pltpu.VMEM((2 * M_HEADS, 8, LANES), F32)],
        compiler_params=pltpu.CompilerParams(dimension_semantics=("parallel", "arbitrary"),
                                             vmem_limit_bytes=VMEM_LIMIT),
        name="mlstm",
    )(mq, mk, mv, gl, glt, mq, mk, mv, gl, glt, km, vm, gtm)


def _attn_kernel(qnt_ref, qrt_ref, kn_ref, kr_ref, vt_ref, knm_ref, krm_ref, vtm_ref, o_ref, *, tk):
    qt = jnp.concatenate([qnt_ref[...], qrt_ref[...]], axis=0)

    def vext(vt):
        return jnp.concatenate([vt, jnp.ones((ONES_ROWS, vt.shape[1]), BF16)], axis=0)

    km = jnp.concatenate([knm_ref[...], krm_ref[...]], axis=1)
    s_meta = _mm(km, qt)
    key = lax.broadcasted_iota(jnp.int32, s_meta.shape, 0)
    s_meta = jnp.where(key < N_META, s_meta, NEG)
    n_chunks = kn_ref.shape[0] // tk

    def scores(j):
        rows = slice(j * tk, (j + 1) * tk)
        k = jnp.concatenate([kn_ref[rows, :], kr_ref[rows, :]], axis=1)
        return _mm(k, qt)

    ahead = 2
    pending = [scores(j) for j in range(min(ahead, n_chunks))]
    m = jnp.max(s_meta, axis=0, keepdims=True)
    acc = None
    for j in range(n_chunks):
        s = pending.pop(0)
        if j + ahead < n_chunks:
            pending.append(scores(j + ahead))
        m_new = jnp.maximum(m, jnp.max(s, axis=0, keepdims=True))
        p = jnp.exp2(s - m_new).astype(BF16)
        v = vext(vt_ref[:, j * tk:(j + 1) * tk])
        if j == 0:
            p = jnp.concatenate([jnp.exp2(s_meta - m_new).astype(BF16), p], axis=0)
            acc = _mm(jnp.concatenate([vext(vtm_ref[...]), v], axis=1), p)
        else:
            acc = jnp.exp2(m - m_new) * acc + _mm(v, p)
        m = m_new
    out_t = acc[:A_DV, :] * (1.0 / acc[A_DV:A_DV + 1, :])
    o_ref[...] = out_t.T.astype(o_ref.dtype)


def _attention(qnt, qrt, kn, kr, vat, meta, batch, tq, tk):
    rows = kn.shape[0]
    seq = rows // batch
    nq = seq // tq
    knm, krm, vatm = meta
    n_meta_pad = knm.shape[0]
    qt_blk = pl.BlockSpec((LANES, tq), lambda b, h, qi: (h, b * nq + qi))
    k_blk = pl.BlockSpec((seq, LANES), lambda b, h, qi: (b, h))
    kr_blk = pl.BlockSpec((seq, LANES), lambda b, h, qi: (b, 0))
    vt_blk = pl.BlockSpec((A_DV, seq), lambda b, h, qi: (h, b))
    meta_h = pl.BlockSpec((n_meta_pad, LANES), lambda b, h, qi: (0, h))
    meta_0 = pl.BlockSpec((n_meta_pad, LANES), lambda b, h, qi: (0, 0))
    meta_vt = pl.BlockSpec((A_DV, n_meta_pad), lambda b, h, qi: (h, 0))
    return pl.pallas_call(
        functools.partial(_attn_kernel, tk=tk),
        grid=(batch, A_HEADS, nq),
        in_specs=[qt_blk, qt_blk, k_blk, kr_blk, vt_blk, meta_h, meta_0, meta_vt],
        out_specs=pl.BlockSpec((tq, A_DV), lambda b, h, qi: (b * nq + qi, h)),
        out_shape=jax.ShapeDtypeStruct((rows, A_HEADS * A_DV), BF16),
        compiler_params=pltpu.CompilerParams(
            dimension_semantics=("parallel", "parallel", "parallel"),
            vmem_limit_bytes=VMEM_LIMIT),
        name="attention",
    )(qnt, qrt, kn, kr, vat, knm, krm, vatm)


def _mix_kernel(x_ref, hf_ref, hb_ref, so_ref, att_ref, sg_ref, gm_ref, wm_ref, wa_ref, wo_ref,
                gffn_ref, wrh_ref, wrl_ref, br_ref, h1_ref, u2_ref, idx_ref, gate_ref, hist_ref, *,
                n_split):
    d = x_ref.shape[1]
    tr = x_ref.shape[0] // n_split
    for part in range(n_split):
        rows = slice(part * tr, (part + 1) * tr)
        hs = hf_ref[rows, :].astype(F32) + hb_ref[rows, :].astype(F32)
        parts = []
        for h in range(M_HEADS):
            blk = hs[:, h * M_DV:(h + 1) * M_DV]
            parts.append(blk * lax.rsqrt(jnp.mean(blk * blk, axis=-1, keepdims=True) + EPS))
        hn = jnp.concatenate(parts, axis=1) * gm_ref[...] * so_ref[rows, :].astype(F32)
        y_m = _mm(hn.astype(BF16), wm_ref[...])
        y_a = _mm(att_ref[rows, :], wa_ref[...])
        sg = sg_ref[rows, :].astype(F32)
        mixed = sg[:, :d] * y_m + sg[:, d:] * y_a
        h1 = x_ref[rows, :] + _mm(mixed.astype(BF16), wo_ref[...])
        h1_ref[rows, :] = h1
        u2 = _rms(h1, gffn_ref[...])
        u2_hi = u2.astype(BF16)
        u2_ref[rows, :] = u2_hi
        u2_lo = (u2 - u2_hi.astype(F32)).astype(BF16)
        logits = (_mm(u2_hi, wrh_ref[...]) + _mm(u2_hi, wrl_ref[...]) + _mm(u2_lo, wrh_ref[...])
                  + br_ref[...])
        lane = lax.broadcasted_iota(jnp.int32, logits.shape, 1).astype(F32)
        out_lane = lax.broadcasted_iota(jnp.int32, (tr, TOP_K), 1)
        work = logits
        top_idx = jnp.zeros((tr, TOP_K), F32)
        top_exp = jnp.zeros((tr, TOP_K), F32)
        v_max = None
        hist = jnp.zeros((1, N_EXPERTS), F32)
        for k in range(TOP_K):
            v = jnp.max(work, axis=-1, keepdims=True)
            idx = jnp.min(jnp.where(work == v, lane, float(N_EXPERTS)), axis=-1, keepdims=True)
            picked = lane == idx
            work = jnp.where(picked, -jnp.inf, work)
            hist = hist + jnp.sum(jnp.where(picked, 1.0, 0.0), axis=0, keepdims=True)
            v_max = v if k == 0 else v_max
            top_idx = jnp.where(out_lane == k, idx, top_idx)
            top_exp = jnp.where(out_lane == k, jnp.exp(v - v_max), top_exp)
        idx_ref[rows, :] = top_idx.astype(jnp.int32)
        gate_ref[rows, :] = top_exp / jnp.sum(top_exp, axis=-1, keepdims=True)
        hist_ref[part] = hist


def _mix(x2d, hf, hb, so, att, sg, weights, tm):
    rows, d = x2d.shape
    row_blk = lambda w: pl.BlockSpec((tm, w), lambda i: (i, 0))
    in_specs = ([row_blk(d), row_blk(d), row_blk(d), row_blk(d), row_blk(d), row_blk(2 * d)]
                + [_resident(w.shape) for w in weights])
    n_split = 2 if tm % 32 == 0 else 1
    return pl.pallas_call(
        functools.partial(_mix_kernel, n_split=n_split),
        grid=(rows // tm,),
        in_specs=in_specs,
        out_specs=[row_blk(d), row_blk(d), row_blk(TOP_K), row_blk(TOP_K),
                   pl.BlockSpec((n_split, 1, N_EXPERTS), lambda i: (i, 0, 0))],
        out_shape=[jax.ShapeDtypeStruct((rows, d), F32), jax.ShapeDtypeStruct((rows, d), BF16),
                   jax.ShapeDtypeStruct((rows, TOP_K), jnp.int32),
                   jax.ShapeDtypeStruct((rows, TOP_K), F32),
                   jax.ShapeDtypeStruct((rows // tm * n_split, 1, N_EXPERTS), F32)],
        compiler_params=pltpu.CompilerParams(dimension_semantics=("parallel",),
                                             vmem_limit_bytes=VMEM_LIMIT),
        name="mix",
    )(x2d, hf, hb, so, att, sg, *weights)


def _expert_kernel(be_ref, first_ref, next_ref, slot_ref, nused_ref,
                   x_ref, w1_hbm, b1_ref, w2_hbm, b2_ref, y_ref,
                   w1f_scr, w2f_scr, w1b_scr, w2b_scr, sem):
    i = pl.program_id(0)

    def weight_copies(e, slot):
        return (pltpu.make_async_copy(w1_hbm.at[e], w1f_scr.at[slot], sem.at[0, slot]),
                pltpu.make_async_copy(w2_hbm.at[e], w2f_scr.at[slot], sem.at[1, slot]))

    @pl.when(i < nused_ref[0])
    def _():
        @pl.when(first_ref[i] == 1)
        def _switch():
            slot = slot_ref[i]

            @pl.when(i == 0)
            def _():
                for c in weight_copies(be_ref[i], slot):
                    c.start()

            @pl.when(next_ref[i] >= 0)
            def _():
                for c in weight_copies(next_ref[i], 1 - slot):
                    c.start()

            for c in weight_copies(be_ref[i], slot):
                c.wait()
            w1b_scr[...] = w1f_scr[slot].astype(BF16)
            w2b_scr[...] = w2f_scr[slot].astype(BF16)

        hid = _mm(x_ref[...], w1b_scr[...]) + b1_ref[...]
        glu = jnp.minimum(hid[:, :D_FF], SWIGLU_LIMIT)
        lin = jnp.clip(hid[:, D_FF:], -SWIGLU_LIMIT, SWIGLU_LIMIT)
        act = glu * _sigmoid(SWIGLU_ALPHA * glu) * (lin + 1.0)
        y_ref[...] = (_mm(act.astype(BF16), w2b_scr[...]) + b2_ref[...]).astype(y_ref.dtype)

    @pl.when(i >= nused_ref[0])
    def _():
        y_ref[...] = jnp.zeros_like(y_ref)


def _experts(block_e, n_used, xg, w1, b1, w2, b2, tme):
    rows, d = xg.shape
    nb = rows // tme
    i32 = jnp.int32
    used = jnp.arange(nb, dtype=i32) < n_used[0]
    first = jnp.concatenate([jnp.ones((1,), i32), (block_e[1:] != block_e[:-1]).astype(i32)])
    later = (block_e[None, :] > block_e[:, None]) & used[None, :]
    nxt = jnp.min(jnp.where(later, block_e[None, :], N_EXPERTS), axis=1)
    nxt = jnp.where(nxt == N_EXPERTS, -1, nxt).astype(i32)
    slot = ((jnp.cumsum(first) - 1) % 2).astype(i32)
    row_map = lambda i, *_: (i, 0)
    bias_map = lambda i, be, *_: (be[i], 0, 0)
    grid_spec = pltpu.PrefetchScalarGridSpec(
        num_scalar_prefetch=5,
        grid=(nb,),
        in_specs=[
            pl.BlockSpec((tme, d), row_map),
            pl.BlockSpec(memory_space=pl.ANY),
            pl.BlockSpec((None, 1, 2 * D_FF), bias_map),
            pl.BlockSpec(memory_space=pl.ANY),
            pl.BlockSpec((None, 1, d), bias_map),
        ],
        out_specs=pl.BlockSpec((tme, d), row_map),
        scratch_shapes=[pltpu.VMEM((2, d, 2 * D_FF), F32), pltpu.VMEM((2, D_FF, d), F32),
                        pltpu.VMEM((d, 2 * D_FF), BF16), pltpu.VMEM((D_FF, d), BF16),
                        pltpu.SemaphoreType.DMA((2, 2))],
    )
    return pl.pallas_call(
        _expert_kernel,
        grid_spec=grid_spec,
        out_shape=jax.ShapeDtypeStruct((rows, d), BF16),
        compiler_params=pltpu.CompilerParams(dimension_semantics=("arbitrary",),
                                             vmem_limit_bytes=VMEM_LIMIT),
        name="experts",
    )(block_e, first, nxt, slot, n_used, xg, w1, b1, w2, b2)


def _final_kernel(h1_ref, gate_ref, g_ref, *refs, tiles_per_stream):
    yg_refs, o_ref = refs[:-1], refs[-1]
    stream = pl.program_id(0) // tiles_per_stream
    gates = gate_ref[...]
    y = h1_ref[...]
    for k in range(TOP_K):
        rows_k = yg_refs[0][k]
        for s in range(1, len(yg_refs)):
            rows_k = jnp.where(stream == s, yg_refs[s][k], rows_k)
        y = y + rows_k.astype(F32) * gates[:, k:k + 1]
    o_ref[...] = _rms(y, g_ref[...])


def _final(h1, ygs, gates, g, tm):
    rows, d = h1.shape
    tiles_per_stream = ygs[0].shape[1] // tm
    row_blk = pl.BlockSpec((tm, d), lambda i: (i, 0))

    def yg_blk(s):
        clamp = lambda i: jnp.clip(i - s * tiles_per_stream, 0, tiles_per_stream - 1)
        return pl.BlockSpec((TOP_K, tm, d), lambda i: (0, clamp(i), 0))

    return pl.pallas_call(
        functools.partial(_final_kernel, tiles_per_stream=tiles_per_stream),
        grid=(rows // tm,),
        in_specs=[row_blk, pl.BlockSpec((tm, TOP_K), lambda i: (i, 0)), pl.BlockSpec((1, d), lambda i: (0, 0))]
        + [yg_blk(s) for s in range(len(ygs))],
        out_specs=row_blk,
        out_shape=jax.ShapeDtypeStruct((rows, d), F32),
        compiler_params=pltpu.CompilerParams(dimension_semantics=("parallel",)),
        name="final",
    )(h1, gates, g, *ygs)


def _rope_tables(pos):
    half = A_ROPE // 2
    freqs = ROPE_THETA ** (-jnp.arange(half, dtype=F32) / half)
    ang = pos.astype(F32)[:, None] * freqs[None, :]
    zeros = jnp.zeros((pos.shape[0], LANES - A_ROPE), F32)
    cos, sin = jnp.cos(ang), jnp.sin(ang)
    return jnp.concatenate([cos, cos, zeros], axis=1), jnp.concatenate([sin, sin, zeros], axis=1)


def _rope_pair(w):
    half = A_ROPE // 2
    zeros = jnp.zeros(w.shape[:-1] + (LANES - A_ROPE,), w.dtype)
    a = jnp.concatenate([w, zeros], axis=-1)
    b = jnp.concatenate([-w[..., half:], w[..., :half], zeros], axis=-1)
    return a, b


def _pick_tile(n, prefs):
    for t in prefs:
        if n % t == 0:
            return t
    return n


def kernel(x, meta_tokens, norm_mix, w_in, m_b_i, m_b_f, m_norm, w_m_proj, a_w_uq, a_w_ukv,
           a_norm_q, a_norm_kv, w_a_proj, w_out, norm_ffn, w_router, b_router, w1, b1, w2, b2,
           norm_final):
    assert norm_mix.shape[0] == 1, "single-layer encoder"
    batch, seq, d = x.shape
    rows = batch * seq
    x2d = x.reshape(rows, d)

    sizes = (M_HEADS * M_DQK, M_HEADS * M_DQK, M_HEADS * M_DV, M_HEADS * M_DV, 2 * M_HEADS,
             2 * M_HEADS, Q_LORA, KV_LORA, A_ROPE, 2 * d)
    offs = np.cumsum((0,) + sizes)
    w_q, w_k, w_v, w_o, w_i, w_f, w_cq, w_ckv, w_kr, w_g = (
        w_in[0][:, offs[i]:offs[i + 1]] for i in range(len(sizes)))
    kr_a, kr_b = _rope_pair(w_kr)
    w_qvt = jnp.concatenate([w_q * (M_DQK ** -0.5), w_v], axis=1).T.astype(BF16)
    w_ko = jnp.concatenate([w_k, w_o], axis=1).astype(BF16)
    w_lat = jnp.concatenate([w_cq, w_ckv, kr_a, kr_b], axis=1).astype(BF16)
    w_if = jnp.concatenate([w_i, w_f], axis=1).astype(BF16)
    b_if = jnp.concatenate([m_b_i[0].reshape(-1), m_b_f[0].reshape(-1)]).astype(F32)
    uq_a, uq_b = _rope_pair(a_w_uq[0][:, :, A_NOPE:])
    w_uqt = jnp.concatenate([a_w_uq[0][:, :, :A_NOPE].reshape(Q_LORA, -1), uq_a.reshape(Q_LORA, -1),
                             uq_b.reshape(Q_LORA, -1)], axis=1).T.astype(BF16)
    w_uk = a_w_ukv[0][:, :, :A_NOPE].reshape(KV_LORA, -1).astype(BF16)
    w_uvt = a_w_ukv[0][:, :, A_NOPE:].reshape(KV_LORA, -1).T.astype(BF16)
    in_weights = (norm_mix[0].reshape(1, d), w_qvt, w_ko, w_g.astype(BF16), w_lat,
                  w_if, w_if.T, b_if.reshape(1, -1), b_if.reshape(-1, 1),
                  a_norm_q[0].reshape(1, -1), a_norm_kv[0].reshape(1, -1), w_uqt, w_uk, w_uvt)

    cos_r, sin_r = _rope_tables(jnp.arange(N_META, N_META + seq))
    cos_m, sin_m = _rope_tables(jnp.arange(N_META))
    tm = _pick_tile(seq, (256, 128, 64, 32, 16))
    (mq, mk, mv, so, sg, gl, glt, qn, qr, kn, va, kr) = _in_proj(
        x2d, cos_r, sin_r, in_weights, _pick_tile(seq, (512, 256, 128, 64, 32, 16)))
    (_, mk_m, mvt_m, _, _, _, glt_m, _, _, kn_m, vat_m, kr_m) = _in_proj(
        meta_tokens.astype(x.dtype), cos_m, sin_m, in_weights, N_META)

    n_pad = LANES - N_META
    pad_gate = jnp.concatenate([jnp.full((n_pad, 2 * M_HEADS), NEG, F32),
                                jnp.zeros((n_pad, 2 * M_HEADS), F32)], axis=1)
    front = lambda a: jnp.concatenate([jnp.zeros((n_pad, a.shape[1]), a.dtype), a], axis=0)
    front_t = lambda a: jnp.concatenate([jnp.zeros((a.shape[0], n_pad), a.dtype), a], axis=1)
    mlstm_meta = (front(mk_m), front_t(mvt_m), jnp.concatenate([pad_gate.T, glt_m], axis=1))
    lc = _pick_tile(seq, (256, 128))
    hf, hb = _mlstm(mq, mk, mv, gl, glt, mlstm_meta, batch, lc)

    back = lambda a: jnp.concatenate([a, jnp.zeros((n_pad, a.shape[1]), a.dtype)], axis=0)
    back_t = lambda a: jnp.concatenate([a, jnp.zeros((a.shape[0], n_pad), a.dtype)], axis=1)
    tq = _pick_tile(seq, ATT_Q_TILES)
    tk = _pick_tile(seq, (512, 256, 128))
    att = _attention(qn, qr, kn, kr, va, (back(kn_m), back(kr_m), back_t(vat_m)), batch, tq, tk)

    wr_hi = w_router[0].astype(BF16)
    wr_lo = (w_router[0] - wr_hi.astype(F32)).astype(BF16)
    mix_weights = (m_norm[0].reshape(1, -1), w_m_proj[0].astype(BF16), w_a_proj[0].astype(BF16),
                   w_out[0].astype(BF16), norm_ffn[0].reshape(1, d), wr_hi, wr_lo,
                   b_router[0].reshape(1, -1))
    mix_tile = _pick_tile(rows, (512, 256, 128, 64, 32, 16))
    h1, u2, top_idx, gates, hist = _mix(x2d, hf, hb, so, att, sg, mix_weights, mix_tile)

    tme = 512
    i32 = jnp.int32
    n_streams = TOKEN_STREAMS if rows % (TOKEN_STREAMS * mix_tile) == 0 else 1
    rs = rows // n_streams
    n = rs * TOP_K
    nb = -(-n // tme) + N_EXPERTS
    hist = hist.reshape(n_streams, -1, N_EXPERTS)
    b1r, b2r = b1[0].reshape(N_EXPERTS, 1, -1), b2[0].reshape(N_EXPERTS, 1, -1)
    ygs = []
    for s in range(n_streams):
        row0 = s * rs
        flat_e = top_idx[row0:row0 + rs].T.reshape(-1)
        pair_ids = jnp.arange(n, dtype=i32)
        _, order = lax.sort((flat_e, pair_ids), num_keys=1, is_stable=True)
        _, rank = lax.sort((order, pair_ids), num_keys=1)
        counts = jnp.sum(hist[s], axis=0).astype(i32)
        start = jnp.cumsum(counts) - counts
        padded = (counts + tme - 1) // tme * tme
        padded_end = jnp.cumsum(padded)
        padded_start = padded_end - padded
        block_row0 = jnp.arange(nb, dtype=i32) * tme
        block_e = jnp.minimum(jnp.sum(padded_end[None, :] <= block_row0[:, None], axis=1, dtype=i32),
                              N_EXPERTS - 1)
        n_used = (padded_end[-1] // tme).astype(i32).reshape(1)
        off = (block_row0 - padded_start[block_e])[:, None] + jnp.arange(tme, dtype=i32)[None, :]
        valid = off < counts[block_e][:, None]
        src = jnp.clip(start[block_e][:, None] + off, 0, n - 1)
        filler = (block_row0[:, None] + jnp.arange(tme, dtype=i32)[None, :]) % rs
        buf_tok = (jnp.where(valid, order[src] % rs, filler) + row0).reshape(-1)
        run_end = start + counts
        slot = rank + jnp.sum(jnp.where(rank[None, :] >= run_end[:, None], (padded - counts)[:, None], 0),
                              axis=0, dtype=i32)
        yb = _experts(block_e, n_used, u2[buf_tok], w1[0], b1r, w2[0], b2r, tme)
        ygs.append(yb[slot].reshape(TOP_K, rs, d))

    out = _final(h1, ygs, gates, norm_final.reshape(1, d), tm)
    return out.reshape(batch, seq, d)
```

```python
import functools

import jax
import jax.numpy as jnp
import numpy as np
from jax import lax
from jax.experimental import pallas as pl
from jax.experimental.pallas import tpu as pltpu

N_META = 16
EPS = 1e-6
NEG = -1e30
M_HEADS = 8
M_DQK = 64
M_DV = 128
A_HEADS = 8
A_NOPE = 128
A_ROPE = 64
A_DV = 128
Q_LORA = 256
KV_LORA = 128
ROPE_THETA = 10000.0
N_EXPERTS = 32
TOP_K = 4
D_FF = 1024
SWIGLU_ALPHA = 1.702
SWIGLU_LIMIT = 7.0

LANES = 128
ATT_Q_TILES = (2048, 1024, 512, 256, 128)
TOKEN_STREAMS = 1
STAGE_SKEW = 2
ONES_ROWS = 16
N_GATE_COLS = 4 * M_HEADS
ATT_SCALE = (A_NOPE + A_ROPE) ** -0.5 * 1.4426950408889634
VMEM_LIMIT = 56 * 1024 * 1024

F32 = jnp.float32
BF16 = jnp.bfloat16


def _mm(a, b):
    return jnp.dot(a, b, preferred_element_type=F32)


def _mm_nt(a, b):
    return lax.dot_general(a, b, (((1,), (1,)), ((), ())), preferred_element_type=F32)


def _mm_tn(a, b):
    return lax.dot_general(a, b, (((0,), (0,)), ((), ())), preferred_element_type=F32)


def _mm_exact(a, b):
    return jnp.dot(a, b, preferred_element_type=F32, precision=lax.Precision.HIGHEST)


def _rms(x, g):
    return x * lax.rsqrt(jnp.mean(x * x, axis=-1, keepdims=True) + EPS) * g


def _log_sigmoid(x):
    return jnp.minimum(x, 0.0) - jnp.log1p(jnp.exp(-jnp.abs(x)))


def _sigmoid(x):
    return 1.0 / (1.0 + jnp.exp(-x))


def _resident(shape):
    nd = len(shape)
    return pl.BlockSpec(shape, lambda *_: (0,) * nd, pipeline_mode=pl.Buffered(1))


def _in_proj_kernel(x_ref, cos_ref, sin_ref, cost_ref, sint_ref, gmix_ref, wqvt_ref, wko_ref, wgate_ref,
                    wlat_ref, wif_ref, wift_ref, bif_ref, bift_ref, gq_ref, gkv_ref, wuqt_ref, wuk_ref,
                    wuvt_ref, mqt_ref, mk_ref, mvt_ref, so_ref, sg_ref, gl_ref, glt_ref,
                    qnt_ref, qrt_ref, kn_ref, vat_ref, kr_ref, *, n_split):
    d_qk = M_HEADS * M_DQK
    d_n = A_HEADS * A_NOPE
    tr = x_ref.shape[0] // n_split
    for part in range(n_split):
        rows = slice(part * tr, (part + 1) * tr)
        ub = _rms(x_ref[rows, :], gmix_ref[...]).astype(BF16)

        qvt = _mm_nt(wqvt_ref[...], ub)
        mqt_ref[:, rows] = qvt[:d_qk, :].astype(BF16)
        mvt_ref[:, rows] = qvt[d_qk:, :].astype(BF16)
        ko = _mm(ub, wko_ref[...])
        mk_ref[rows, :] = ko[:, :d_qk].astype(BF16)
        so_ref[rows, :] = _sigmoid(ko[:, d_qk:]).astype(BF16)
        sg_ref[rows, :] = _sigmoid(_mm(ub, wgate_ref[...])).astype(BF16)

        gi = _mm(ub, wif_ref[...]) + bif_ref[...]
        col = lax.broadcasted_iota(jnp.int32, gi.shape, 1)
        gl_ref[rows, :] = jnp.where(col < 2 * M_HEADS, gi, _log_sigmoid(gi))
        git = _mm_nt(wift_ref[...], ub) + bift_ref[...]
        row = lax.broadcasted_iota(jnp.int32, git.shape, 0)
        glt_ref[:, rows] = jnp.where(row < 2 * M_HEADS, git, _log_sigmoid(git))

        lat = _mm(ub, wlat_ref[...])
        kr_ref[rows, :] = (lat[:, Q_LORA + KV_LORA:Q_LORA + KV_LORA + LANES] * cos_ref[rows, :]
                           + lat[:, Q_LORA + KV_LORA + LANES:] * sin_ref[rows, :]).astype(BF16)

        cqn = _rms(lat[:, :Q_LORA], gq_ref[...]).astype(BF16)
        q3t = _mm_nt(wuqt_ref[...], cqn)
        qnt_ref[:, rows] = (q3t[:d_n, :] * ATT_SCALE).astype(BF16)
        cos_t = cost_ref[:, rows]
        sin_t = sint_ref[:, rows]
        for h in range(A_HEADS):
            a = q3t[d_n + h * LANES:d_n + (h + 1) * LANES, :]
            b = q3t[d_n + (A_HEADS + h) * LANES:d_n + (A_HEADS + h + 1) * LANES, :]
            qrt_ref[h * LANES:(h + 1) * LANES, rows] = ((a * cos_t + b * sin_t) * ATT_SCALE).astype(BF16)

        ckvn = _rms(lat[:, Q_LORA:Q_LORA + KV_LORA], gkv_ref[...]).astype(BF16)
        kn_ref[rows, :] = _mm(ckvn, wuk_ref[...]).astype(BF16)
        vat_ref[:, rows] = _mm_nt(wuvt_ref[...], ckvn).astype(BF16)


def _in_proj(x2d, cos4, sin4, weights, tm):
    rows, d = x2d.shape
    n_tab = cos4.shape[0] // tm
    row_blk = lambda w: pl.BlockSpec((tm, w), lambda i: (i, 0))
    tab_blk = pl.BlockSpec((tm, LANES), lambda i: (i % n_tab, 0))
    tab_t_blk = pl.BlockSpec((LANES, tm), lambda i: (0, i % n_tab))
    d_qk, d_v, d_n = M_HEADS * M_DQK, M_HEADS * M_DV, A_HEADS * A_NOPE
    out_shape = [
        jax.ShapeDtypeStruct((d_qk, rows), BF16),
        jax.ShapeDtypeStruct((rows, d_qk), BF16),
        jax.ShapeDtypeStruct((d_v, rows), BF16),
        jax.ShapeDtypeStruct((rows, d_v), BF16),
        jax.ShapeDtypeStruct((rows, 2 * d), BF16),
        jax.ShapeDtypeStruct((rows, N_GATE_COLS), F32),
        jax.ShapeDtypeStruct((N_GATE_COLS, rows), F32),
        jax.ShapeDtypeStruct((d_n, rows), BF16),
        jax.ShapeDtypeStruct((A_HEADS * LANES, rows), BF16),
        jax.ShapeDtypeStruct((rows, d_n), BF16),
        jax.ShapeDtypeStruct((A_HEADS * A_DV, rows), BF16),
        jax.ShapeDtypeStruct((rows, LANES), BF16),
    ]
    col_blk = lambda h: pl.BlockSpec((h, tm), lambda i: (0, i))
    out_specs = [col_blk(d_qk), row_blk(d_qk), col_blk(d_v), row_blk(d_v), row_blk(2 * d),
                 row_blk(N_GATE_COLS), col_blk(N_GATE_COLS),
                 col_blk(d_n), col_blk(A_HEADS * LANES), row_blk(d_n), col_blk(A_HEADS * A_DV),
                 row_blk(LANES)]
    in_specs = ([row_blk(d), tab_blk, tab_blk, tab_t_blk, tab_t_blk]
                + [_resident(w.shape) for w in weights])
    return pl.pallas_call(
        functools.partial(_in_proj_kernel, n_split=2 if tm % 256 == 0 else 1),
        grid=(rows // tm,),
        in_specs=in_specs,
        out_specs=out_specs,
        out_shape=out_shape,
        compiler_params=pltpu.CompilerParams(dimension_semantics=("parallel",),
                                             vmem_limit_bytes=VMEM_LIMIT),
        name="in_proj",
    )(x2d, cos4, sin4, cos4.T, sin4.T, *weights)


def _mlstm_kernel(qf_ref, kf_ref, vf_ref, gf_ref, gtf_ref,
                  qb_ref, kb_ref, vb_ref, gb_ref, gtb_ref,
                  km_ref, vm_ref, gtm_ref,
                  hf_ref, hb_ref, c_scr, m_scr):
    c_idx = pl.program_id(1)
    lc = kf_ref.shape[0]
    lm = km_ref.shape[0]
    n_pairs = M_HEADS // 2
    half = LANES // 2
    log2e = 1.4426950408889634

    def tri(n, lower):
        r = lax.broadcasted_iota(jnp.int32, (n, n), 0)
        c = lax.broadcasted_iota(jnp.int32, (n, n), 1)
        return (r >= c) if lower else (r <= c)

    def as_bf16(mask):
        return jnp.where(mask, 1.0, 0.0).astype(BF16)

    def eye(n):
        r = lax.broadcasted_iota(jnp.int32, (n, n), 0)
        c = lax.broadcasted_iota(jnp.int32, (n, n), 1)
        return as_bf16(r == c)

    eye_t = eye(lc)
    lane = lax.broadcasted_iota(jnp.int32, (1, LANES), 1)
    sub128 = lax.broadcasted_iota(jnp.int32, (LANES, 1), 0)

    def lane_mask(h):
        return (lane < half) if h % 2 == 0 else (lane >= half)

    def sublane_mask(h):
        return (sub128 < half) if h % 2 == 0 else (sub128 >= half)

    def split3(x):
        hi = x.astype(BF16)
        r1 = x - hi.astype(F32)
        mid = r1.astype(BF16)
        lo = (r1 - mid.astype(F32)).astype(BF16)
        return hi, mid, lo

    def cum_rows(g_rows, tri_b):
        hi, mid, lo = split3(g_rows)
        return _mm(hi, tri_b) + _mm(mid, tri_b) + _mm(lo, tri_b)

    def cum_cols(tri_b, g_cols):
        hi, mid, lo = split3(g_cols)
        return _mm(tri_b, hi) + _mm(tri_b, mid) + _mm(tri_b, lo)

    def state_update(direction, pair, k2, vt_pair, cum_row, g_rows, tot_col):
        n = k2.shape[0]
        d_c = None
        decays = []
        for sub in range(2):
            h = 2 * pair + sub
            j = direction * M_HEADS + h
            jf = 2 * M_HEADS + j
            m_prev = m_scr[j][0:1, 0:1]
            tot = tot_col[jf:jf + 1, :]
            g_row = tot - cum_row[jf:jf + 1, :] + g_rows[j:j + 1, :]
            m_new = jnp.maximum(tot + m_prev, jnp.max(g_row, axis=1, keepdims=True))
            decays.append(jnp.exp2(tot + m_prev - m_new))
            wk = jnp.exp2(g_row - m_new)
            vw = jnp.concatenate([(vt_pair[sub].astype(F32) * wk).astype(BF16),
                                  jnp.broadcast_to(wk.astype(BF16), (ONES_ROWS, n))], axis=0)
            km = jnp.where(lane_mask(h), k2, jnp.zeros_like(k2))
            upd = _mm(vw, km)
            d_c = upd if d_c is None else d_c + upd
            m_scr[j] = jnp.broadcast_to(m_new, m_scr.shape[1:])
        decay_cols = jnp.where(lane < half, decays[0], decays[1])
        ci = direction * n_pairs + pair
        c_scr[ci] = decay_cols * c_scr[ci] + d_c

    @pl.when(c_idx == 0)
    def _init():
        c_scr[...] = jnp.zeros_like(c_scr)
        m_scr[...] = jnp.zeros_like(m_scr)
        g_rows = gtm_ref[...] * log2e
        cum_m = cum_rows(g_rows, as_bf16(tri(lm, False)))
        tot_m = cum_m[:, lm - 1:lm]
        for pair in range(n_pairs):
            k2 = km_ref[:, pair * LANES:(pair + 1) * LANES]
            vt_pair = [vm_ref[(2 * pair + s) * M_DV:(2 * pair + s + 1) * M_DV, :] for s in range(2)]
            state_update(0, pair, k2, vt_pair, cum_m, g_rows, tot_m)

    views = ((qf_ref, kf_ref, vf_ref, gf_ref, gtf_ref, hf_ref),
             (qb_ref, kb_ref, vb_ref, gb_ref, gtb_ref, hb_ref))
    shared = []
    for direction, (_, _, _, g_ref, gt_ref, _) in enumerate(views):
        fwd = direction == 0
        g_cols = g_ref[...] * log2e
        g_rows = gt_ref[...] * log2e
        cum_row = cum_rows(g_rows, as_bf16(tri(lc, not fwd)))
        cum_col = cum_cols(as_bf16(tri(lc, fwd)), g_cols)
        tot_col = cum_row[:, lc - 1:lc] if fwd else cum_row[:, 0:1]
        key_cols = g_cols[:, :2 * M_HEADS] - cum_col[:, 2 * M_HEADS:]
        valid = tri(lc, not fwd)
        shared.append((g_rows, cum_row, tot_col, key_cols, valid))

    units = [(d, p, s) for d in range(2) for p in range(n_pairs) for s in range(2)]
    pair_vals = {}
    pair_out = {}
    vals = {}

    def scores(u):
        direction, pair, sub = units[u]
        q_ref, k_ref, v_ref = views[direction][:3]
        h = 2 * pair + sub
        if sub == 0:
            q2t = q_ref[pair * LANES:(pair + 1) * LANES, :]
            k2 = k_ref[:, pair * LANES:(pair + 1) * LANES]
            c_b = c_scr[direction * n_pairs + pair].astype(BF16)
            pair_vals[(direction, pair)] = (q2t, k2, c_b, [])
        q2t, k2, _, vt_pair = pair_vals[(direction, pair)]
        vt = v_ref[h * M_DV:(h + 1) * M_DV, :]
        vt_pair.append(vt)
        qmt = jnp.where(sublane_mask(h), q2t, jnp.zeros_like(q2t))
        vals[u] = dict(vt=vt, qmt=qmt, st=_mm(k2, qmt))

    def accumulate(u):
        direction, pair, sub = units[u]
        g_rows, cum_row, tot_col, key_cols, valid = shared[direction]
        _, k2, c_b, vt_pair = pair_vals[(direction, pair)]
        j = direction * M_HEADS + 2 * pair + sub
        jf = 2 * M_HEADS + j
        v = vals[u]
        b_row = cum_row[jf:jf + 1, :]
        dt = jnp.where(valid, key_cols[:, j:j + 1] + b_row, NEG)
        m_inter = b_row + m_scr[j][0:1, 0:1]
        m_t = jnp.maximum(m_inter, jnp.max(dt, axis=0, keepdims=True))
        wt = (jnp.exp2(dt - m_t) * v['st']).astype(BF16)
        qs = (v['qmt'].astype(F32) * jnp.exp2(m_inter - m_t)).astype(BF16)
        vext = jnp.concatenate([v['vt'], jnp.ones((ONES_ROWS, lc), BF16)], axis=0)
        v['acc'] = _mm(vext, wt) + _mm(c_b, qs)
        v['m_t'] = m_t
        if sub == 1:
            state_update(direction, pair, k2, vt_pair, cum_row, g_rows, tot_col)

    def finish(u):
        direction, pair, sub = units[u]
        h_ref = views[direction][5]
        v = vals.pop(u)
        den = jnp.maximum(jnp.abs(v['acc'][M_DV:M_DV + 1, :]), jnp.exp2(-v['m_t']))
        ht = (v['acc'][:M_DV, :] * (1.0 / den)).astype(BF16)
        if sub == 0:
            pair_out[(direction, pair)] = ht
        else:
            hts = jnp.concatenate([pair_out.pop((direction, pair)), ht], axis=0)
            h_ref[:, 2 * pair * M_DV:2 * (pair + 1) * M_DV] = _mm_nt(eye_t, hts).astype(h_ref.dtype)

    for t in range(len(units) + 2 * STAGE_SKEW):
        if t < len(units):
            scores(t)
        if 0 <= t - STAGE_SKEW < len(units):
            accumulate(t - STAGE_SKEW)
        if 0 <= t - 2 * STAGE_SKEW < len(units):
            finish(t - 2 * STAGE_SKEW)


def _mlstm(mq, mk, mv, gl, glt, meta, batch, lc):
    rows = mk.shape[0]
    seq = rows // batch
    nc = seq // lc
    km, vm, gtm = meta
    d_qk, d_v = M_HEADS * M_DQK, M_HEADS * M_DV

    fwd_row = lambda b, c: (b * nc + c, 0)
    bwd_row = lambda b, c: (b * nc + nc - 1 - c, 0)
    fwd_col = lambda b, c: (0, b * nc + c)
    bwd_col = lambda b, c: (0, b * nc + nc - 1 - c)

    def views(row_map, col_map):
        return [pl.BlockSpec((d_qk, lc), col_map), pl.BlockSpec((lc, d_qk), row_map),
                pl.BlockSpec((d_v, lc), col_map), pl.BlockSpec((lc, N_GATE_COLS), row_map),
                pl.BlockSpec((N_GATE_COLS, lc), col_map)]

    const = lambda a: pl.BlockSpec(a.shape, lambda b, c: (0, 0))
    in_specs = views(fwd_row, fwd_col) + views(bwd_row, bwd_col) + [const(a) for a in meta]
    return pl.pallas_call(
        _mlstm_kernel,
        grid=(batch, nc),
        in_specs=in_specs,
        out_specs=[pl.BlockSpec((lc, d_v), fwd_row), pl.BlockSpec((lc, d_v), bwd_row)],
        out_shape=[jax.ShapeDtypeStruct((rows, d_v), BF16)] * 2,
        scratch_shapes=[pltpu.VMEM((2 * (M_HEADS // 2), M_DV + ONES_ROWS, LANES), F32),
                        pltpu.VMEM((2 * M_HEADS, 8, LANES), F32)],
        compiler_params=pltpu.CompilerParams(dimension_semantics=("parallel", "arbitrary"),
                                             vmem_limit_bytes=VMEM_LIMIT),
        name="mlstm",
    )(mq, mk, mv, gl, glt, mq, mk, mv, gl, glt, km, vm, gtm)


def _attn_kernel(qnt_ref, qrt_ref, kn_ref, kr_ref, vt_ref, knm_ref, krm_ref, vtm_ref, o_ref, *, tk):
    qt = jnp.concatenate([qnt_ref[...], qrt_ref[...]], axis=0)

    def vext(vt):
        return jnp.concatenate([vt, jnp.ones((ONES_ROWS, vt.shape[1]), BF16)], axis=0)

    km = jnp.concatenate([knm_ref[...], krm_ref[...]], axis=1)
    s_meta = _mm(km, qt)
    key = lax.broadcasted_iota(jnp.int32, s_meta.shape, 0)
    s_meta = jnp.where(key < N_META, s_meta, NEG)
    n_chunks = kn_ref.shape[0] // tk

    def scores(j):
        rows = slice(j * tk, (j + 1) * tk)
        k = jnp.concatenate([kn_ref[rows, :], kr_ref[rows, :]], axis=1)
        return _mm(k, qt)

    ahead = 2
    pending = [scores(j) for j in range(min(ahead, n_chunks))]
    m = jnp.max(s_meta, axis=0, keepdims=True)
    acc = None
    for j in range(n_chunks):
        s = pending.pop(0)
        if j + ahead < n_chunks:
            pending.append(scores(j + ahead))
        m_new = jnp.maximum(m, jnp.max(s, axis=0, keepdims=True))
        p = jnp.exp2(s - m_new).astype(BF16)
        v = vext(vt_ref[:, j * tk:(j + 1) * tk])
        if j == 0:
            p = jnp.concatenate([jnp.exp2(s_meta - m_new).astype(BF16), p], axis=0)
            acc = _mm(jnp.concatenate([vext(vtm_ref[...]), v], axis=1), p)
        else:
            acc = jnp.exp2(m - m_new) * acc + _mm(v, p)
        m = m_new
    out_t = acc[:A_DV, :] * (1.0 / acc[A_DV:A_DV + 1, :])
    o_ref[...] = out_t.T.astype(o_ref.dtype)


def _attention(qnt, qrt, kn, kr, vat, meta, batch, tq, tk):
    rows = kn.shape[0]
    seq = rows // batch
    nq = seq // tq
    knm, krm, vatm = meta
    n_meta_pad = knm.shape[0]
    qt_blk = pl.BlockSpec((LANES, tq), lambda b, h, qi: (h, b * nq + qi))
    k_blk = pl.BlockSpec((seq, LANES), lambda b, h, qi: (b, h))
    kr_blk = pl.BlockSpec((seq, LANES), lambda b, h, qi: (b, 0))
    vt_blk = pl.BlockSpec((A_DV, seq), lambda b, h, qi: (h, b))
    meta_h = pl.BlockSpec((n_meta_pad, LANES), lambda b, h, qi: (0, h))
    meta_0 = pl.BlockSpec((n_meta_pad, LANES), lambda b, h, qi: (0, 0))
    meta_vt = pl.BlockSpec((A_DV, n_meta_pad), lambda b, h, qi: (h, 0))
    return pl.pallas_call(
        functools.partial(_attn_kernel, tk=tk),
        grid=(batch, A_HEADS, nq),
        in_specs=[qt_blk, qt_blk, k_blk, kr_blk, vt_blk, meta_h, meta_0, meta_vt],
        out_specs=pl.BlockSpec((tq, A_DV), lambda b, h, qi: (b * nq + qi, h)),
        out_shape=jax.ShapeDtypeStruct((rows, A_HEADS * A_DV), BF16),
        compiler_params=pltpu.CompilerParams(
            dimension_semantics=("parallel", "parallel", "parallel"),
            vmem_limit_bytes=VMEM_LIMIT),
        name="attention",
    )(qnt, qrt, kn, kr, vat, knm, krm, vatm)


def _mix_kernel(x_ref, hf_ref, hb_ref, so_ref, att_ref, sg_ref, gm_ref, wm_ref, wa_ref, wo_ref,
                gffn_ref, wrh_ref, wrl_ref, br_ref, h1_ref, u2_ref, idx_ref, gate_ref, hist_ref, *,
                n_split):
    d = x_ref.shape[1]
    tr = x_ref.shape[0] // n_split
    for part in range(n_split):
        rows = slice(part * tr, (part + 1) * tr)
        hs = hf_ref[rows, :].astype(F32) + hb_ref[rows, :].astype(F32)
        parts = []
        for h in range(M_HEADS):
            blk = hs[:, h * M_DV:(h + 1) * M_DV]
            parts.append(blk * lax.rsqrt(jnp.mean(blk * blk, axis=-1, keepdims=True) + EPS))
        hn = jnp.concatenate(parts, axis=1) * gm_ref[...] * so_ref[rows, :].astype(F32)
        y_m = _mm(hn.astype(BF16), wm_ref[...])
        y_a = _mm(att_ref[rows, :], wa_ref[...])
        sg = sg_ref[rows, :].astype(F32)
        mixed = sg[:, :d] * y_m + sg[:, d:] * y_a
        h1 = x_ref[rows, :] + _mm(mixed.astype(BF16), wo_ref[...])
        h1_ref[rows, :] = h1
        u2 = _rms(h1, gffn_ref[...])
        u2_hi = u2.astype(BF16)
        u2_ref[rows, :] = u2_hi
        u2_lo = (u2 - u2_hi.astype(F32)).astype(BF16)
        logits = (_mm(u2_hi, wrh_ref[...]) + _mm(u2_hi, wrl_ref[...]) + _mm(u2_lo, wrh_ref[...])
                  + br_ref[...])
        lane = lax.broadcasted_iota(jnp.int32, logits.shape, 1).astype(F32)
        out_lane = lax.broadcasted_iota(jnp.int32, (tr, TOP_K), 1)
        work = logits
        top_idx = jnp.zeros((tr, TOP_K), F32)
        top_exp = jnp.zeros((tr, TOP_K), F32)
        v_max = None
        hist = jnp.zeros((1, N_EXPERTS), F32)
        for k in range(TOP_K):
            v = jnp.max(work, axis=-1, keepdims=True)
            idx = jnp.min(jnp.where(work == v, lane, float(N_EXPERTS)), axis=-1, keepdims=True)
            picked = lane == idx
            work = jnp.where(picked, -jnp.inf, work)
            hist = hist + jnp.sum(jnp.where(picked, 1.0, 0.0), axis=0, keepdims=True)
            v_max = v if k == 0 else v_max
            top_idx = jnp.where(out_lane == k, idx, top_idx)
            top_exp = jnp.where(out_lane == k, jnp.exp(v - v_max), top_exp)
        idx_ref[rows, :] = top_idx.astype(jnp.int32)
        gate_ref[rows, :] = top_exp / jnp.sum(top_exp, axis=-1, keepdims=True)
        hist_ref[part] = hist


def _mix(x2d, hf, hb, so, att, sg, weights, tm):
    rows, d = x2d.shape
    row_blk = lambda w: pl.BlockSpec((tm, w), lambda i: (i, 0))
    in_specs = ([row_blk(d), row_blk(d), row_blk(d), row_blk(d), row_blk(d), row_blk(2 * d)]
                + [_resident(w.shape) for w in weights])
    n_split = 2 if tm % 32 == 0 else 1
    return pl.pallas_call(
        functools.partial(_mix_kernel, n_split=n_split),
        grid=(rows // tm,),
        in_specs=in_specs,
        out_specs=[row_blk(d), row_blk(d), row_blk(TOP_K), row_blk(TOP_K),
                   pl.BlockSpec((n_split, 1, N_EXPERTS), lambda i: (i, 0, 0))],
        out_shape=[jax.ShapeDtypeStruct((rows, d), F32), jax.ShapeDtypeStruct((rows, d), BF16),
                   jax.ShapeDtypeStruct((rows, TOP_K), jnp.int32),
                   jax.ShapeDtypeStruct((rows, TOP_K), F32),
                   jax.ShapeDtypeStruct((rows // tm * n_split, 1, N_EXPERTS), F32)],
        compiler_params=pltpu.CompilerParams(dimension_semantics=("parallel",),
                                             vmem_limit_bytes=VMEM_LIMIT),
        name="mix",
    )(x2d, hf, hb, so, att, sg, *weights)


def _expert_kernel(be_ref, first_ref, next_ref, slot_ref, nused_ref,
                   x_ref, w1_hbm, b1_ref, w2_hbm, b2_ref, y_ref,
                   w1f_scr, w2f_scr, w1b_scr, w2b_scr, sem):
    i = pl.program_id(0)

    def weight_copies(e, slot):
        return (pltpu.make_async_copy(w1_hbm.at[e], w1f_scr.at[slot], sem.at[0, slot]),
                pltpu.make_async_copy(w2_hbm.at[e], w2f_scr.at[slot], sem.at[1, slot]))

    @pl.when(i < nused_ref[0])
    def _():
        @pl.when(first_ref[i] == 1)
        def _switch():
            slot = slot_ref[i]

            @pl.when(i == 0)
            def _():
                for c in weight_copies(be_ref[i], slot):
                    c.start()

            @pl.when(next_ref[i] >= 0)
            def _():
                for c in weight_copies(next_ref[i], 1 - slot):
                    c.start()

            for c in weight_copies(be_ref[i], slot):
                c.wait()
            w1b_scr[...] = w1f_scr[slot].astype(BF16)
            w2b_scr[...] = w2f_scr[slot].astype(BF16)

        hid = _mm(x_ref[...], w1b_scr[...]) + b1_ref[...]
        glu = jnp.minimum(hid[:, :D_FF], SWIGLU_LIMIT)
        lin = jnp.clip(hid[:, D_FF:], -SWIGLU_LIMIT, SWIGLU_LIMIT)
        act = glu * _sigmoid(SWIGLU_ALPHA * glu) * (lin + 1.0)
        y_ref[...] = (_mm(act.astype(BF16), w2b_scr[...]) + b2_ref[...]).astype(y_ref.dtype)

    @pl.when(i >= nused_ref[0])
    def _():
        y_ref[...] = jnp.zeros_like(y_ref)


def _experts(block_e, n_used, xg, w1, b1, w2, b2, tme):
    rows, d = xg.shape
    nb = rows // tme
    i32 = jnp.int32
    used = jnp.arange(nb, dtype=i32) < n_used[0]
    first = jnp.concatenate([jnp.ones((1,), i32), (block_e[1:] != block_e[:-1]).astype(i32)])
    later = (block_e[None, :] > block_e[:, None]) & used[None, :]
    nxt = jnp.min(jnp.where(later, block_e[None, :], N_EXPERTS), axis=1)
    nxt = jnp.where(nxt == N_EXPERTS, -1, nxt).astype(i32)
    slot = ((jnp.cumsum(first) - 1) % 2).astype(i32)
    row_map = lambda i, *_: (i, 0)
    bias_map = lambda i, be, *_: (be[i], 0, 0)
    grid_spec = pltpu.PrefetchScalarGridSpec(
        num_scalar_prefetch=5,
        grid=(nb,),
        in_specs=[
            pl.BlockSpec((tme, d), row_map),
            pl.BlockSpec(memory_space=pl.ANY),
            pl.BlockSpec((None, 1, 2 * D_FF), bias_map),
            pl.BlockSpec(memory_space=pl.ANY),
            pl.BlockSpec((None, 1, d), bias_map),
        ],
        out_specs=pl.BlockSpec((tme, d), row_map),
        scratch_shapes=[pltpu.VMEM((2, d, 2 * D_FF), F32), pltpu.VMEM((2, D_FF, d), F32),
                        pltpu.VMEM((d, 2 * D_FF), BF16), pltpu.VMEM((D_FF, d), BF16),
                        pltpu.SemaphoreType.DMA((2, 2))],
    )
    return pl.pallas_call(
        _expert_kernel,
        grid_spec=grid_spec,
        out_shape=jax.ShapeDtypeStruct((rows, d), BF16),
        compiler_params=pltpu.CompilerParams(dimension_semantics=("arbitrary",),
                                             vmem_limit_bytes=VMEM_LIMIT),
        name="experts",
    )(block_e, first, nxt, slot, n_used, xg, w1, b1, w2, b2)


def _final_kernel(h1_ref, gate_ref, g_ref, *refs, tiles_per_stream):
    yg_refs, o_ref = refs[:-1], refs[-1]
    stream = pl.program_id(0) // tiles_per_stream
    gates = gate_ref[...]
    y = h1_ref[...]
    for k in range(TOP_K):
        rows_k = yg_refs[0][k]
        for s in range(1, len(yg_refs)):
            rows_k = jnp.where(stream == s, yg_refs[s][k], rows_k)
        y = y + rows_k.astype(F32) * gates[:, k:k + 1]
    o_ref[...] = _rms(y, g_ref[...])


def _final(h1, ygs, gates, g, tm):
    rows, d = h1.shape
    tiles_per_stream = ygs[0].shape[1] // tm
    row_blk = pl.BlockSpec((tm, d), lambda i: (i, 0))

    def yg_blk(s):
        clamp = lambda i: jnp.clip(i - s * tiles_per_stream, 0, tiles_per_stream - 1)
        return pl.BlockSpec((TOP_K, tm, d), lambda i: (0, clamp(i), 0))

    return pl.pallas_call(
        functools.partial(_final_kernel, tiles_per_stream=tiles_per_stream),
        grid=(rows // tm,),
        in_specs=[row_blk, pl.BlockSpec((tm, TOP_K), lambda i: (i, 0)), pl.BlockSpec((1, d), lambda i: (0, 0))]
        + [yg_blk(s) for s in range(len(ygs))],
        out_specs=row_blk,
        out_shape=jax.ShapeDtypeStruct((rows, d), F32),
        compiler_params=pltpu.CompilerParams(dimension_semantics=("parallel",)),
        name="final",
    )(h1, gates, g, *ygs)


def _rope_tables(pos):
    half = A_ROPE // 2
    freqs = ROPE_THETA ** (-jnp.arange(half, dtype=F32) / half)
    ang = pos.astype(F32)[:, None] * freqs[None, :]
    zeros = jnp.zeros((pos.shape[0], LANES - A_ROPE), F32)
    cos, sin = jnp.cos(ang), jnp.sin(ang)
    return jnp.concatenate([cos, cos, zeros], axis=1), jnp.concatenate([sin, sin, zeros], axis=1)


def _rope_pair(w):
    half = A_ROPE // 2
    zeros = jnp.zeros(w.shape[:-1] + (LANES - A_ROPE,), w.dtype)
    a = jnp.concatenate([w, zeros], axis=-1)
    b = jnp.concatenate([-w[..., half:], w[..., :half], zeros], axis=-1)
    return a, b


def _pick_tile(n, prefs):
    for t in prefs:
        if n % t == 0:
            return t
    return n


def kernel(x, meta_tokens, norm_mix, w_in, m_b_i, m_b_f, m_norm, w_m_proj, a_w_uq, a_w_ukv,
           a_norm_q, a_norm_kv, w_a_proj, w_out, norm_ffn, w_router, b_router, w1, b1, w2, b2,
           norm_final):
    assert norm_mix.shape[0] == 1, "single-layer encoder"
    batch, seq, d = x.shape
    rows = batch * seq
    x2d = x.reshape(rows, d)

    sizes = (M_HEADS * M_DQK, M_HEADS * M_DQK, M_HEADS * M_DV, M_HEADS * M_DV, 2 * M_HEADS,
             2 * M_HEADS, Q_LORA, KV_LORA, A_ROPE, 2 * d)
    offs = np.cumsum((0,) + sizes)
    w_q, w_k, w_v, w_o, w_i, w_f, w_cq, w_ckv, w_kr, w_g = (
        w_in[0][:, offs[i]:offs[i + 1]] for i in range(len(sizes)))
    kr_a, kr_b = _rope_pair(w_kr)
    w_qvt = jnp.concatenate([w_q * (M_DQK ** -0.5), w_v], axis=1).T.astype(BF16)
    w_ko = jnp.concatenate([w_k, w_o], axis=1).astype(BF16)
    w_lat = jnp.concatenate([w_cq, w_ckv, kr_a, kr_b], axis=1).astype(BF16)
    w_if = jnp.concatenate([w_i, w_f], axis=1).astype(BF16)
    b_if = jnp.concatenate([m_b_i[0].reshape(-1), m_b_f[0].reshape(-1)]).astype(F32)
    uq_a, uq_b = _rope_pair(a_w_uq[0][:, :, A_NOPE:])
    w_uqt = jnp.concatenate([a_w_uq[0][:, :, :A_NOPE].reshape(Q_LORA, -1), uq_a.reshape(Q_LORA, -1),
                             uq_b.reshape(Q_LORA, -1)], axis=1).T.astype(BF16)
    w_uk = a_w_ukv[0][:, :, :A_NOPE].reshape(KV_LORA, -1).astype(BF16)
    w_uvt = a_w_ukv[0][:, :, A_NOPE:].reshape(KV_LORA, -1).T.astype(BF16)
    in_weights = (norm_mix[0].reshape(1, d), w_qvt, w_ko, w_g.astype(BF16), w_lat,
                  w_if, w_if.T, b_if.reshape(1, -1), b_if.reshape(-1, 1),
                  a_norm_q[0].reshape(1, -1), a_norm_kv[0].reshape(1, -1), w_uqt, w_uk, w_uvt)

    cos_r, sin_r = _rope_tables(jnp.arange(N_META, N_META + seq))
    cos_m, sin_m = _rope_tables(jnp.arange(N_META))
    tm = _pick_tile(seq, (256, 128, 64, 32, 16))
    (mq, mk, mv, so, sg, gl, glt, qn, qr, kn, va, kr) = _in_proj(
        x2d, cos_r, sin_r, in_weights, _pick_tile(seq, (512, 256, 128, 64, 32, 16)))
    (_, mk_m, mvt_m, _, _, _, glt_m, _, _, kn_m, vat_m, kr_m) = _in_proj(
        meta_tokens.astype(x.dtype), cos_m, sin_m, in_weights, N_META)

    n_pad = LANES - N_META
    pad_gate = jnp.concatenate([jnp.full((n_pad, 2 * M_HEADS), NEG, F32),
                                jnp.zeros((n_pad, 2 * M_HEADS), F32)], axis=1)
    front = lambda a: jnp.concatenate([jnp.zeros((n_pad, a.shape[1]), a.dtype), a], axis=0)
    front_t = lambda a: jnp.concatenate([jnp.zeros((a.shape[0], n_pad), a.dtype), a], axis=1)
    mlstm_meta = (front(mk_m), front_t(mvt_m), jnp.concatenate([pad_gate.T, glt_m], axis=1))
    lc = _pick_tile(seq, (256, 128))
    hf, hb = _mlstm(mq, mk, mv, gl, glt, mlstm_meta, batch, lc)

    back = lambda a: jnp.concatenate([a, jnp.zeros((n_pad, a.shape[1]), a.dtype)], axis=0)
    back_t = lambda a: jnp.concatenate([a, jnp.zeros((a.shape[0], n_pad), a.dtype)], axis=1)
    tq = _pick_tile(seq, ATT_Q_TILES)
    tk = _pick_tile(seq, (512, 256, 128))
    att = _attention(qn, qr, kn, kr, va, (back(kn_m), back(kr_m), back_t(vat_m)), batch, tq, tk)

    wr_hi = w_router[0].astype(BF16)
    wr_lo = (w_router[0] - wr_hi.astype(F32)).astype(BF16)
    mix_weights = (m_norm[0].reshape(1, -1), w_m_proj[0].astype(BF16), w_a_proj[0].astype(BF16),
                   w_out[0].astype(BF16), norm_ffn[0].reshape(1, d), wr_hi, wr_lo,
                   b_router[0].reshape(1, -1))
    mix_tile = _pick_tile(rows, (512, 256, 128, 64, 32, 16))
    h1, u2, top_idx, gates, hist = _mix(x2d, hf, hb, so, att, sg, mix_weights, mix_tile)

    tme = 512
    i32 = jnp.int32
    n_streams = TOKEN_STREAMS if rows % (TOKEN_STREAMS * mix_tile) == 0 else 1
    rs = rows // n_streams
    n = rs * TOP_K
    nb = -(-n // tme) + N_EXPERTS
    hist = hist.reshape(n_streams, -1, N_EXPERTS)
    b1r, b2r = b1[0].reshape(N_EXPERTS, 1, -1), b2[0].reshape(N_EXPERTS, 1, -1)
    ygs = []
    for s in range(n_streams):
        row0 = s * rs
        flat_e = top_idx[row0:row0 + rs].T.reshape(-1)
        pair_ids = jnp.arange(n, dtype=i32)
        _, order = lax.sort((flat_e, pair_ids), num_keys=1, is_stable=True)
        _, rank = lax.sort((order, pair_ids), num_keys=1)
        counts = jnp.sum(hist[s], axis=0).astype(i32)
        start = jnp.cumsum(counts) - counts
        padded = (counts + tme - 1) // tme * tme
        padded_end = jnp.cumsum(padded)
        padded_start = padded_end - padded
        block_row0 = jnp.arange(nb, dtype=i32) * tme
        block_e = jnp.minimum(jnp.sum(padded_end[None, :] <= block_row0[:, None], axis=1, dtype=i32),
                              N_EXPERTS - 1)
        n_used = (padded_end[-1] // tme).astype(i32).reshape(1)
        off = (block_row0 - padded_start[block_e])[:, None] + jnp.arange(tme, dtype=i32)[None, :]
        valid = off < counts[block_e][:, None]
        src = jnp.clip(start[block_e][:, None] + off, 0, n - 1)
        filler = (block_row0[:, None] + jnp.arange(tme, dtype=i32)[None, :]) % rs
        buf_tok = (jnp.where(valid, order[src] % rs, filler) + row0).reshape(-1)
        run_end = start + counts
        slot = rank + jnp.sum(jnp.where(rank[None, :] >= run_end[:, None], (padded - counts)[:, None], 0),
                              axis=0, dtype=i32)
        yb = _experts(block_e, n_used, u2[buf_tok], w1[0], b1r, w2[0], b2r, tme)
        ygs.append(yb[slot].reshape(TOP_K, rs, d))

    out = _final(h1, ygs, gates, norm_final.reshape(1, d), tm)
    return out.reshape(batch, seq, d)
```
